```python
import jax, jax.numpy as jnp
from jax import lax
import numpy as np


D_MODEL = 1024
BATCH = 8
SEQ = 2048
DEPTH = 2

MLA_HEADS = 8
MLA_NOPE = 64
MLA_ROPE = 32
MLA_V = 64
Q_LORA = 384
KV_LORA = 256
ROPE_BASE = 10000.0
DSA_HEADS = 8
DSA_DIM = 64
IDX_HEADS = 8
IDX_DIM = 32
TOPK_MAX = 256
D_MIX = MLA_HEADS * MLA_V + DSA_HEADS * DSA_DIM
N_EXPERTS = 16
N_GROUPS = 4
EXPERTS_PER_GROUP = N_EXPERTS // N_GROUPS
TOP_K_EXPERTS = 2
GROUP_SCORE_K = 2
D_FF_EXPERT = 512
ALPHA = (2.0 * DEPTH) ** 0.25
BETA = (8.0 * DEPTH) ** -0.25
Q_BLOCK = 128
LN_EPS = 1e-5
RMS_EPS = 1e-6

SPLIT_SIZES = (Q_LORA, KV_LORA, MLA_ROPE, DSA_HEADS * DSA_DIM, DSA_DIM, DSA_DIM,
               IDX_HEADS * IDX_DIM, IDX_DIM, IDX_HEADS)
SPLIT_POINTS = tuple(int(v) for v in np.cumsum(SPLIT_SIZES)[:-1])
IN_COLS = int(sum(SPLIT_SIZES))
DSA_V_START = int(sum(SPLIT_SIZES[:5]))

kernel_name = 'hybrid_mla_dsa_grouped_moe_deepnorm'


def layer_norm(x, g, b):
    xf = x.astype(jnp.float32)
    mu = jnp.mean(xf, axis=-1, keepdims=True)
    var = jnp.mean(jnp.square(xf - mu), axis=-1, keepdims=True)
    y = (xf - mu) * lax.rsqrt(var + LN_EPS) * g.astype(jnp.float32) + b.astype(jnp.float32)
    return y.astype(x.dtype)


def rms_norm(x, g):
    xf = x.astype(jnp.float32)
    y = xf * lax.rsqrt(jnp.mean(jnp.square(xf), axis=-1, keepdims=True) + RMS_EPS)
    return (y * g.astype(jnp.float32)).astype(x.dtype)


def rope(x, pos):
    half = x.shape[-1] // 2
    inv = ROPE_BASE ** (-jnp.arange(half, dtype=jnp.float32) / half)
    ang = pos[:, None] * inv[None, :]
    cos = jnp.cos(ang)[None, :, None, :]
    sin = jnp.sin(ang)[None, :, None, :]
    xf = x.astype(jnp.float32)
    x1, x2 = xf[..., :half], xf[..., half:]
    return jnp.concatenate([x1 * cos - x2 * sin, x1 * sin + x2 * cos], axis=-1).astype(x.dtype)


def sweep_query_blocks(block_fn, seq_len):
    starts = jnp.arange(seq_len // Q_BLOCK, dtype=jnp.int32) * Q_BLOCK
    out = lax.map(block_fn, starts)
    nb, b, qb, h, d = out.shape
    return jnp.transpose(out, (1, 0, 2, 3, 4)).reshape(b, nb * qb, h * d)


def mla_group(qa, kva, kr, q_norm_g, w_q_up, kv_norm_g, w_uk, w_uv, pos):
    B, T, _ = qa.shape
    q = (rms_norm(qa, q_norm_g) @ w_q_up).reshape(B, T, MLA_HEADS, MLA_NOPE + MLA_ROPE)
    q = jnp.concatenate([q[..., :MLA_NOPE], rope(q[..., MLA_NOPE:], pos)], axis=-1)
    c_kv = rms_norm(kva, kv_norm_g)
    k_nope = (c_kv @ w_uk).reshape(B, T, MLA_HEADS, MLA_NOPE)
    v = (c_kv @ w_uv).reshape(B, T, MLA_HEADS, MLA_V)
    k_rope = rope(kr[:, :, None, :], pos)
    k = jnp.concatenate([k_nope, jnp.broadcast_to(k_rope, (B, T, MLA_HEADS, MLA_ROPE))], axis=-1)
    scale = (MLA_NOPE + MLA_ROPE) ** -0.5
    key_idx = jnp.arange(T, dtype=jnp.int32)

    def block(start):
        qb = lax.dynamic_slice_in_dim(q, start, Q_BLOCK, axis=1)
        s = jnp.einsum('bthd,bshd->bhts', qb, k).astype(jnp.float32) * scale
        tq = start + jnp.arange(Q_BLOCK, dtype=jnp.int32)
        s = jnp.where(key_idx[None, :] <= tq[:, None], s, -jnp.inf)
        p = jax.nn.softmax(s, axis=-1).astype(v.dtype)
        return jnp.einsum('bhts,bshd->bthd', p, v)

    return sweep_query_blocks(block, T)


def dsa_group(dq, dk, dv, iq, ik, iw, slopes, n_keep):
    B, T, _ = dq.shape
    q = dq.reshape(B, T, DSA_HEADS, DSA_DIM)
    iq = iq.reshape(B, T, IDX_HEADS, IDX_DIM)
    iw = iw * (IDX_HEADS ** -0.5)
    key_idx = jnp.arange(T, dtype=jnp.int32)
    scale = DSA_DIM ** -0.5
    gather = jax.vmap(lambda arr, idx: arr[idx])

    def block(start):
        qb = lax.dynamic_slice_in_dim(q, start, Q_BLOCK, axis=1)
        iqb = lax.dynamic_slice_in_dim(iq, start, Q_BLOCK, axis=1)
        iwb = lax.dynamic_slice_in_dim(iw, start, Q_BLOCK, axis=1)
        tq = start + jnp.arange(Q_BLOCK, dtype=jnp.int32)
        rel = jax.nn.relu(jnp.einsum('bthd,bsd->bths', iqb, ik) * (IDX_DIM ** -0.5))
        score = jnp.einsum('bths,bth->bts', rel, iwb).astype(jnp.float32)
        score = jnp.where(key_idx[None, None, :] <= tq[None, :, None], score, -jnp.inf)
        _, sel = lax.top_k(score, n_keep)
        k_sel = gather(dk, sel)
        v_sel = gather(dv, sel)
        dist = (tq[None, :, None] - sel).astype(jnp.float32)
        s = jnp.einsum('bthd,btkd->bhtk', qb, k_sel).astype(jnp.float32) * scale
        s = s - slopes[None, :, None, None] * dist[:, None, :, :]
        s = jnp.where((dist >= 0.0)[:, None, :, :], s, -jnp.inf)
        p = jax.nn.softmax(s, axis=-1).astype(v_sel.dtype)
        return jnp.einsum('bhtk,btkd->bthd', p, v_sel)

    return sweep_query_blocks(block, T)


def hybrid_mixer(h, w_in, q_norm_g, w_q_up, kv_norm_g, w_uk, w_uv, w_o, pos, slopes, n_keep):
    proj = h @ w_in
    qa, kva, kr, dq, dk, dv, iq, ik, iw = jnp.split(proj, SPLIT_POINTS, axis=-1)
    out_a = mla_group(qa, kva, kr, q_norm_g, w_q_up, kv_norm_g, w_uk, w_uv, pos)
    out_b = dsa_group(dq, dk, dv, iq, ik, iw, slopes, n_keep)
    return jnp.concatenate([out_a, out_b], axis=-1) @ w_o


def grouped_moe(h, router_w, router_bias, w_gate, w_up, w_down):
    B, T, D = h.shape
    xt = h.reshape(B * T, D)
    aff = jax.nn.sigmoid((xt @ router_w).astype(jnp.float32))
    biased = aff + router_bias.astype(jnp.float32)
    grouped = biased.reshape(-1, N_GROUPS, EXPERTS_PER_GROUP)
    group_score = jnp.sum(lax.top_k(grouped, GROUP_SCORE_K)[0], axis=-1)
    g_sel = jnp.argmax(group_score, axis=-1)
    in_group = (jnp.arange(N_EXPERTS) // EXPERTS_PER_GROUP)[None, :] == g_sel[:, None]
    _, e_idx = lax.top_k(jnp.where(in_group, biased, -jnp.inf), TOP_K_EXPERTS)
    wts = jnp.take_along_axis(aff, e_idx, axis=-1)
    wts = wts / jnp.sum(wts, axis=-1, keepdims=True)
    gates = jnp.sum(jax.nn.one_hot(e_idx, N_EXPERTS, dtype=jnp.float32) * wts[..., None], axis=1)
    hg = jnp.einsum('nd,edf->nef', xt, w_gate)
    hu = jnp.einsum('nd,edf->nef', xt, w_up)
    act = jax.nn.silu(hg) * hu * gates[:, :, None].astype(hg.dtype)
    out = jnp.einsum('nef,efd->nd', act, w_down)
    return out.reshape(B, T, D)


def setup_inputs(seed: int = 0) -> dict:
    key = jax.random.key(seed)
    ks = jax.random.split(key, 20)
    f32 = jnp.float32
    nrm = lambda k, shape, s: jax.random.normal(k, shape, f32) * s
    col_scale = np.ones((IN_COLS,), np.float32)
    col_scale[DSA_V_START:DSA_V_START + DSA_DIM] = BETA
    w_in = nrm(ks[1], (DEPTH, D_MODEL, IN_COLS), D_MODEL ** -0.5) * jnp.asarray(col_scale)
    return {
        'x': jax.random.normal(ks[0], (BATCH, SEQ, D_MODEL), f32),
        'w_in': w_in,
        'q_norm_g': 1.0 + nrm(ks[2], (DEPTH, Q_LORA), 0.02),
        'w_q_up': nrm(ks[3], (DEPTH, Q_LORA, MLA_HEADS * (MLA_NOPE + MLA_ROPE)), Q_LORA ** -0.5),
        'kv_norm_g': 1.0 + nrm(ks[4], (DEPTH, KV_LORA), 0.02),
        'w_uk': nrm(ks[5], (DEPTH, KV_LORA, MLA_HEADS * MLA_NOPE), KV_LORA ** -0.5),
        'w_uv': nrm(ks[6], (DEPTH, KV_LORA, MLA_HEADS * MLA_V), BETA * KV_LORA ** -0.5),
        'w_o': nrm(ks[7], (DEPTH, D_MIX, D_MODEL), BETA * D_MIX ** -0.5),
        'ln1_g': 1.0 + nrm(ks[8], (DEPTH, D_MODEL), 0.02),
        'ln1_b': nrm(ks[9], (DEPTH, D_MODEL), 0.02),
        'router_w': nrm(ks[10], (D_MODEL, N_EXPERTS), D_MODEL ** -0.5),
        'router_bias': nrm(ks[11], (N_EXPERTS,), 0.01),
        'w_gate': nrm(ks[12], (DEPTH, N_EXPERTS, D_MODEL, D_FF_EXPERT), BETA * D_MODEL ** -0.5),
        'w_up': nrm(ks[13], (DEPTH, N_EXPERTS, D_MODEL, D_FF_EXPERT), BETA * D_MODEL ** -0.5),
        'w_down': nrm(ks[14], (DEPTH, N_EXPERTS, D_FF_EXPERT, D_MODEL), BETA * D_FF_EXPERT ** -0.5),
        'ln2_g': 1.0 + nrm(ks[15], (DEPTH, D_MODEL), 0.02),
        'ln2_b': nrm(ks[16], (DEPTH, D_MODEL), 0.02),
    }


def reference(x, w_in, q_norm_g, w_q_up, kv_norm_g, w_uk, w_uv, w_o, ln1_g, ln1_b,
              router_w, router_bias, w_gate, w_up, w_down, ln2_g, ln2_b):
    T = x.shape[1]
    pos = jnp.arange(T, dtype=jnp.float32)
    n_keep = min(TOPK_MAX, T // 4)
    slopes = 2.0 ** (-8.0 * jnp.arange(1, DSA_HEADS + 1, dtype=jnp.float32) / DSA_HEADS)
    for l in range(DEPTH):
        mix = hybrid_mixer(x, w_in[l], q_norm_g[l], w_q_up[l], kv_norm_g[l], w_uk[l], w_uv[l],
                           w_o[l], pos, slopes, n_keep)
        x = layer_norm(ALPHA * x + mix, ln1_g[l], ln1_b[l])
        ffn = grouped_moe(x, router_w, router_bias, w_gate[l], w_up[l], w_down[l])
        x = layer_norm(ALPHA * x + ffn, ln2_g[l], ln2_b[l])
    return x
```

```python
import jax
import jax.numpy as jnp
import numpy as np
from jax import lax
from jax.experimental import pallas as pl
from jax.experimental.pallas import tpu as pltpu

F32 = jnp.float32
BF16 = jnp.bfloat16
I32 = jnp.int32

D_MODEL = 1024
DEPTH = 2
MLA_HEADS = 8
MLA_NOPE = 64
MLA_ROPE = 32
MLA_V = 64
Q_LORA = 384
KV_LORA = 256
ROPE_BASE = 10000.0
DSA_HEADS = 8
DSA_DIM = 64
IDX_HEADS = 8
IDX_DIM = 32
TOPK_MAX = 256
N_EXPERTS = 16
N_GROUPS = 4
EXPERTS_PER_GROUP = 4
D_FF = 512
ALPHA = (2.0 * DEPTH) ** 0.25
LN_EPS = 1e-5
RMS_EPS = 1e-6
MLA_SCALE = (MLA_NOPE + MLA_ROPE) ** -0.5
SLOPES = tuple(2.0 ** (-8.0 * (h + 1) / DSA_HEADS) for h in range(DSA_HEADS))

LANE = 128
HEAD_PAD = LANE
NEG = -1e30

C_QA = 0
C_KVA = C_QA + Q_LORA
C_DQ = C_KVA + KV_LORA
C_IQ = C_DQ + DSA_HEADS * DSA_DIM
C_KR = C_IQ + IDX_HEADS * IDX_DIM
C_KV = C_KR + LANE
C_IX = C_KV + LANE
C_END = C_IX + LANE

TM_PROJ = 512
TQ_MLA = 256
TQ_DSA = 128
TK_DSA = 256
TM_MOE = 1024

NT_DIMS = (((1,), (1,)), ((), ()))


def _rms(x, g):
    return x * lax.rsqrt(jnp.mean(x * x, axis=-1, keepdims=True) + RMS_EPS) * g


def _layer_norm(y, g, b):
    mu = jnp.mean(y, axis=-1, keepdims=True)
    d = y - mu
    var = jnp.mean(d * d, axis=-1, keepdims=True)
    return d * lax.rsqrt(var + LN_EPS) * g + b


def _proj_kernel(x_ref, wcat_ref, qg_ref, kvg_ref, wq_ref, wk_ref, wv_ref, rc_ref, rs1_ref, rs2_ref,
                 q_ref, k_ref, v_ref, dq_ref, dk_ref, dv_ref, iq_ref, ik_ref, iw_ref):
    xb = x_ref[0].astype(BF16)
    proj = jnp.dot(xb, wcat_ref[...], preferred_element_type=F32)
    qn = _rms(proj[:, C_QA:C_QA + Q_LORA], qg_ref[...]).astype(BF16)
    ckv = _rms(proj[:, C_KVA:C_KVA + KV_LORA], kvg_ref[...]).astype(BF16)
    rc, rs1, rs2 = rc_ref[...], rs1_ref[...], rs2_ref[...]

    def rope(t):
        return t * rc + pltpu.roll(t, 16, 1) * rs1 + pltpu.roll(t, LANE - 16, 1) * rs2

    kr = rope(proj[:, C_KR:C_KR + LANE])
    for h in range(MLA_HEADS):
        qh = jnp.dot(qn, wq_ref[h], preferred_element_type=F32)
        q_ref[0, h] = rope(qh).astype(BF16)
        kh = jnp.dot(ckv, wk_ref[h], preferred_element_type=F32) + kr
        k_ref[0, h] = kh.astype(BF16)
        v_ref[0, h] = jnp.dot(ckv, wv_ref[h], preferred_element_type=F32).astype(BF16)
    for h in range(DSA_HEADS):
        dq_ref[0, h] = (proj[:, C_DQ + DSA_DIM * h:C_DQ + DSA_DIM * (h + 1)] * 0.125).astype(BF16)
    for h in range(IDX_HEADS):
        iq_ref[0, h] = proj[:, C_IQ + IDX_DIM * h:C_IQ + IDX_DIM * (h + 1)].astype(BF16)
    dk_ref[0] = proj[:, C_KV:C_KV + DSA_DIM].astype(BF16)
    dv_ref[0] = proj[:, C_KV + DSA_DIM:C_KV + 2 * DSA_DIM].astype(BF16)
    ik_ref[0] = proj[:, C_IX:C_IX + IDX_DIM].astype(BF16)
    iw_ref[0] = proj[:, C_IX + IDX_DIM:C_IX + IDX_DIM + IDX_HEADS] * 0.0625


def _proj_call(x, wcat, qg, kvg, wq, wk, wv, rc, rs1, rs2):
    B, T, _ = x.shape
    tm = TM_PROJ
    full = lambda shape: pl.BlockSpec(shape, lambda b, t: (0,) * len(shape))
    head_out = lambda d: pl.BlockSpec((1, MLA_HEADS, tm, d), lambda b, t: (b, 0, t, 0))
    tok_out = lambda d: pl.BlockSpec((1, tm, d), lambda b, t: (b, t, 0))
    tab = pl.BlockSpec((tm, LANE), lambda b, t: (t, 0))
    return pl.pallas_call(
        _proj_kernel,
        grid=(B, T // tm),
        in_specs=[
            pl.BlockSpec((1, tm, D_MODEL), lambda b, t: (b, t, 0)),
            full(wcat.shape), full(qg.shape), full(kvg.shape), full(wq.shape), full(wk.shape), full(wv.shape),
            tab, tab, tab,
        ],
        out_specs=[head_out(HEAD_PAD), head_out(HEAD_PAD), head_out(MLA_V), head_out(DSA_DIM),
                   tok_out(DSA_DIM), tok_out(DSA_DIM), head_out(IDX_DIM), tok_out(IDX_DIM), tok_out(IDX_HEADS)],
        out_shape=[
            jax.ShapeDtypeStruct((B, MLA_HEADS, T, HEAD_PAD), BF16),
            jax.ShapeDtypeStruct((B, MLA_HEADS, T, HEAD_PAD), BF16),
            jax.ShapeDtypeStruct((B, MLA_HEADS, T, MLA_V), BF16),
            jax.ShapeDtypeStruct((B, DSA_HEADS, T, DSA_DIM), BF16),
            jax.ShapeDtypeStruct((B, T, DSA_DIM), BF16),
            jax.ShapeDtypeStruct((B, T, DSA_DIM), BF16),
            jax.ShapeDtypeStruct((B, IDX_HEADS, T, IDX_DIM), BF16),
            jax.ShapeDtypeStruct((B, T, IDX_DIM), BF16),
            jax.ShapeDtypeStruct((B, T, IDX_HEADS), F32),
        ],
        compiler_params=pltpu.CompilerParams(dimension_semantics=("parallel", "parallel")),
        name="proj",
    )(x, wcat, qg, kvg, wq, wk, wv, rc, rs1, rs2)


def _mla_kernel(q_ref, k_ref, v_ref, o_ref):
    i = pl.program_id(1)
    tq = TQ_MLA
    row = i * tq + lax.broadcasted_iota(I32, (tq, tq), 0)
    col = lax.broadcasted_iota(I32, (tq, tq), 1)
    for h in range(MLA_HEADS):
        qh = q_ref[0, h]

        def body(j, carry, h=h, qh=qh):
            m, l, acc = carry
            start = pl.multiple_of(j * tq, tq)
            kc = k_ref[0, h, pl.ds(start, tq), :]
            vc = v_ref[0, h, pl.ds(start, tq), :]
            s = lax.dot_general(qh, kc, NT_DIMS, preferred_element_type=F32) * MLA_SCALE
            s = jnp.where(col + j * tq <= row, s, NEG)
            m_new = jnp.maximum(m, jnp.max(s, axis=-1, keepdims=True))
            p = jnp.exp(s - m_new)
            a = jnp.exp(m - m_new)
            l = a * l + jnp.sum(p, axis=-1, keepdims=True)
            acc = a * acc + jnp.dot(p.astype(BF16), vc, preferred_element_type=F32)
            return m_new, l, acc

        init = (jnp.full((tq, 1), NEG, F32), jnp.zeros((tq, 1), F32), jnp.zeros((tq, MLA_V), F32))
        _, l, acc = lax.fori_loop(0, i + 1, body, init)
        o_ref[0, :, MLA_V * h:MLA_V * (h + 1)] = (acc / l).astype(BF16)


def _mla_call(q, k, v):
    B, H, T, _ = q.shape
    tq = TQ_MLA
    return pl.pallas_call(
        _mla_kernel,
        grid=(B, T // tq),
        in_specs=[
            pl.BlockSpec((1, H, tq, HEAD_PAD), lambda b, i: (b, 0, i, 0)),
            pl.BlockSpec((1, H, T, HEAD_PAD), lambda b, i: (b, 0, 0, 0)),
            pl.BlockSpec((1, H, T, MLA_V), lambda b, i: (b, 0, 0, 0)),
        ],
        out_specs=pl.BlockSpec((1, tq, H * MLA_V), lambda b, i: (b, i, 0)),
        out_shape=jax.ShapeDtypeStruct((B, T, H * MLA_V), BF16),
        compiler_params=pltpu.CompilerParams(dimension_semantics=("parallel", "arbitrary")),
        name="mla",
    )(q, k, v)


def _dsa_kernel(dq_ref, iq_ref, iw_ref, dk_ref, dv_ref, ik_ref, o_ref, sc_ref, m_ref, l_ref, acc_ref):
    i = pl.program_id(1)
    tq, tk = TQ_DSA, TK_DSA
    nch = lax.shift_right_logical(i, 1) + 1
    rows = DSA_HEADS * tq
    row_t = i * tq + lax.broadcasted_iota(I32, (tq, tk), 0)
    col = lax.broadcasted_iota(I32, (tq, tk), 1)
    t_col = i * tq + lax.broadcasted_iota(I32, (tq, 1), 0)

    iq2 = iq_ref[0].reshape(rows, IDX_DIM)
    w = iw_ref[0]

    def score_body(c, carry):
        start = pl.multiple_of(c * tk, tk)
        ikc = ik_ref[0, pl.ds(start, tk), :]
        rel = lax.dot_general(iq2, ikc, NT_DIMS, preferred_element_type=F32)
        rel = jnp.maximum(rel, 0.0).reshape(IDX_HEADS, tq, tk)
        sc = rel[0] * w[:, 0:1]
        for h in range(1, IDX_HEADS):
            sc = sc + rel[h] * w[:, h:h + 1]
        sc_ref[c] = jnp.where(col + c * tk <= row_t, sc, -jnp.inf)
        return carry

    lax.fori_loop(0, nch, score_body, 0)

    def count(pred):
        def body(c, acc):
            return acc + jnp.where(pred(sc_ref[c], col + c * tk), 1.0, 0.0)

        acc = lax.fori_loop(0, nch, body, jnp.zeros((tq, tk), F32))
        return jnp.sum(acc, axis=-1, keepdims=True)

    int_min = jnp.int32(-2 ** 31)

    def key_to_f32(ku):
        ks = ku ^ int_min
        bits = jnp.where(ks >= 0, ks, ks ^ jnp.int32(0x7FFFFFFF))
        return lax.bitcast_convert_type(bits, F32)

    def bis_body(step, res):
        cand = res | lax.shift_left(jnp.int32(1), 31 - step)
        tau_c = key_to_f32(cand)
        cnt = count(lambda sc, idx: sc >= tau_c)
        return jnp.where(cnt >= float(TOPK_MAX), cand, res)

    res = lax.fori_loop(0, 32, bis_body, jnp.zeros((tq, 1), I32))
    tau = jnp.where(t_col < TOPK_MAX - 1, -jnp.inf, key_to_f32(res))

    need = float(TOPK_MAX) - count(lambda sc, idx: sc > tau)

    def tie_body(step, jm):
        cand = jm | lax.shift_left(jnp.int32(1), 10 - step)
        cnt = count(lambda sc, idx: (sc >= tau) & jnp.logical_not(sc > tau) & (idx < cand))
        return jnp.where(cnt < need, cand, jm)

    j_max = lax.fori_loop(0, 11, tie_body, jnp.zeros((tq, 1), I32))

    q2 = dq_ref[0].reshape(rows, DSA_DIM)
    m_ref[...] = jnp.full((rows, 1), NEG, F32)
    l_ref[...] = jnp.zeros((rows, 1), F32)
    acc_ref[...] = jnp.zeros((rows, DSA_DIM), F32)

    def att_body(c, carry):
        start = pl.multiple_of(c * tk, tk)
        kc = dk_ref[0, pl.ds(start, tk), :]
        vc = dv_ref[0, pl.ds(start, tk), :]
        s3 = lax.dot_general(q2, kc, NT_DIMS, preferred_element_type=F32).reshape(DSA_HEADS, tq, tk)
        sc = sc_ref[c]
        idx = col + c * tk
        sel = ((sc > tau) | ((sc >= tau) & (idx <= j_max))) & (sc > -jnp.inf)
        dist = (row_t - idx).astype(F32)
        s = jnp.concatenate([jnp.where(sel, s3[h] - SLOPES[h] * dist, NEG) for h in range(DSA_HEADS)], axis=0)
        m_prev = m_ref[...]
        m_new = jnp.maximum(m_prev, jnp.max(s, axis=-1, keepdims=True))
        p = jnp.exp(s - m_new)
        a = jnp.exp(m_prev - m_new)
        l_ref[...] = a * l_ref[...] + jnp.sum(p, axis=-1, keepdims=True)
        acc_ref[...] = a * acc_ref[...] + jnp.dot(p.astype(BF16), vc, preferred_element_type=F32)
        m_ref[...] = m_new
        return carry

    lax.fori_loop(0, nch, att_body, 0)
    o = acc_ref[...] / l_ref[...]
    for h in range(DSA_HEADS):
        o_ref[0, :, DSA_DIM * h:DSA_DIM * (h + 1)] = o[tq * h:tq * (h + 1)].astype(BF16)


def _dsa_call(dq, iq, iw, dk, dv, ik):
    B, H, T, _ = dq.shape
    tq, tk = TQ_DSA, TK_DSA
    rows = H * tq
    return pl.pallas_call(
        _dsa_kernel,
        grid=(B, T // tq),
        in_specs=[
            pl.BlockSpec((1, H, tq, DSA_DIM), lambda b, i: (b, 0, i, 0)),
            pl.BlockSpec((1, H, tq, IDX_DIM), lambda b, i: (b, 0, i, 0)),
            pl.BlockSpec((1, tq, IDX_HEADS), lambda b, i: (b, i, 0)),
            pl.BlockSpec((1, T, DSA_DIM), lambda b, i: (b, 0, 0)),
            pl.BlockSpec((1, T, DSA_DIM), lambda b, i: (b, 0, 0)),
            pl.BlockSpec((1, T, IDX_DIM), lambda b, i: (b, 0, 0)),
        ],
        out_specs=pl.BlockSpec((1, tq, H * DSA_DIM), lambda b, i: (b, i, 0)),
        out_shape=jax.ShapeDtypeStruct((B, T, H * DSA_DIM), BF16),
        scratch_shapes=[
            pltpu.VMEM((T // tk, tq, tk), F32),
            pltpu.VMEM((rows, 1), F32),
            pltpu.VMEM((rows, 1), F32),
            pltpu.VMEM((rows, DSA_DIM), F32),
        ],
        compiler_params=pltpu.CompilerParams(dimension_semantics=("parallel", "arbitrary")),
        name="dsa",
    )(dq, iq, iw, dk, dv, ik)


def _route(aff, biased):
    b = [biased[e:e + 1] for e in range(N_EXPERTS)]
    a = [aff[e:e + 1] for e in range(N_EXPERTS)]
    gs = []
    for g in range(N_GROUPS):
        v0, v1, v2, v3 = b[4 * g:4 * g + 4]
        hi1, lo1 = jnp.maximum(v0, v1), jnp.minimum(v0, v1)
        hi2, lo2 = jnp.maximum(v2, v3), jnp.minimum(v2, v3)
        gs.append(jnp.maximum(hi1, hi2) + jnp.maximum(jnp.minimum(hi1, hi2), jnp.maximum(lo1, lo2)))
    best, sel = gs[0], jnp.zeros_like(gs[0], dtype=I32)
    for g in range(1, N_GROUPS):
        upd = gs[g] > best
        sel = jnp.where(upd, g, sel)
        best = jnp.where(upd, gs[g], best)
    v, av = [], []
    for k in range(EXPERTS_PER_GROUP):
        vk, ak = b[k], a[k]
        for g in range(1, N_GROUPS):
            vk = jnp.where(sel == g, b[4 * g + k], vk)
            ak = jnp.where(sel == g, a[4 * g + k], ak)
        v.append(vk)
        av.append(ak)
    m1, i1 = v[0], jnp.zeros_like(sel)
    for k in range(1, EXPERTS_PER_GROUP):
        upd = v[k] > m1
        i1 = jnp.where(upd, k, i1)
        m1 = jnp.where(upd, v[k], m1)
    m2, i2 = jnp.full_like(m1, -jnp.inf), jnp.zeros_like(sel)
    for k in range(EXPERTS_PER_GROUP):
        cand = jnp.where(i1 == k, -jnp.inf, v[k])
        upd = cand > m2
        i2 = jnp.where(upd, k, i2)
        m2 = jnp.where(upd, cand, m2)
    a1, a2 = jnp.zeros_like(m1), jnp.zeros_like(m1)
    for k in range(EXPERTS_PER_GROUP):
        a1 = jnp.where(i1 == k, av[k], a1)
        a2 = jnp.where(i2 == k, av[k], a2)
    tot = a1 + a2
    w1, w2 = a1 / tot, a2 / tot
    gates = []
    for e in range(N_EXPERTS):
        g, k = divmod(e, EXPERTS_PER_GROUP)
        in_g = sel == g
        gates.append(jnp.where(in_g & (i1 == k), w1, 0.0) + jnp.where(in_g & (i2 == k), w2, 0.0))
    return gates


def _out_kernel(a_ref, b_ref, x_ref, wo_ref, g_ref, beta_ref, rwh_ref, rwl_ref, rb_ref, x1_ref, gates_ref):
    half = MLA_HEADS * MLA_V
    mix = jnp.dot(a_ref[0], wo_ref[0:half, :], preferred_element_type=F32)
    mix = mix + jnp.dot(b_ref[0], wo_ref[half:2 * half, :], preferred_element_type=F32)
    x1 = _layer_norm(ALPHA * x_ref[0] + mix, g_ref[...], beta_ref[...])
    x1_ref[0] = x1
    xh = x1.astype(BF16)
    xl = (x1 - xh.astype(F32)).astype(BF16)
    rwh, rwl = rwh_ref[...], rwl_ref[...]
    logits = lax.dot_general(rwh, xh, NT_DIMS, preferred_element_type=F32)
    logits = logits + lax.dot_general(rwh, xl, NT_DIMS, preferred_element_type=F32)
    logits = logits + lax.dot_general(rwl, xh, NT_DIMS, preferred_element_type=F32)
    aff = 1.0 / (1.0 + jnp.exp(-logits))
    gates = _route(aff, aff + rb_ref[...])
    tm = logits.shape[1]
    gt = jnp.concatenate(gates + [jnp.zeros((LANE - N_EXPERTS, tm), F32)], axis=0)
    gates_ref[0] = gt.T


def _out_call(a, b, x, wo, g, beta, rwh, rwl, rb):
    B, T, _ = x.shape
    tm = TM_PROJ
    full = lambda shape: pl.BlockSpec(shape, lambda bb, t: (0,) * len(shape))
    tok = lambda d: pl.BlockSpec((1, tm, d), lambda bb, t: (bb, t, 0))
    return pl.pallas_call(
        _out_kernel,
        grid=(B, T // tm),
        in_specs=[tok(a.shape[-1]), tok(b.shape[-1]), tok(D_MODEL), full(wo.shape), full(g.shape), full(beta.shape),
                  full(rwh.shape), full(rwl.shape), full(rb.shape)],
        out_specs=[tok(D_MODEL), tok(LANE)],
        out_shape=[jax.ShapeDtypeStruct((B, T, D_MODEL), F32), jax.ShapeDtypeStruct((B, T, LANE), F32)],
        compiler_params=pltpu.CompilerParams(dimension_semantics=("parallel", "parallel")),
        name="outproj",
    )(a, b, x, wo, g, beta, rwh, rwl, rb)


def _moe_kernel(x_ref, gates_ref, wg_ref, wu_ref, wd_ref, g_ref, beta_ref, o_ref, acc_ref, xb_ref):
    e = pl.program_id(1)

    @pl.when(e == 0)
    def _():
        acc_ref[...] = jnp.zeros_like(acc_ref)
        xb_ref[...] = x_ref[...].astype(BF16)

    xb = xb_ref[...]
    hg = jnp.dot(xb, wg_ref[0], preferred_element_type=F32)
    hu = jnp.dot(xb, wu_ref[0], preferred_element_type=F32)
    gates = gates_ref[...]
    lane = lax.broadcasted_iota(I32, gates.shape, 1)
    gate = jnp.sum(jnp.where(lane == e, gates, 0.0), axis=-1, keepdims=True)
    act = (hg / (1.0 + jnp.exp(-hg))) * hu * gate
    acc_ref[...] += jnp.dot(act.astype(BF16), wd_ref[0], preferred_element_type=F32)

    @pl.when(e == N_EXPERTS - 1)
    def _():
        o_ref[...] = _layer_norm(ALPHA * x_ref[...] + acc_ref[...], g_ref[...], beta_ref[...])


def _moe_call(x1, gates, wg, wu, wd, g, beta):
    n = x1.shape[0]
    tm = TM_MOE
    vec = pl.BlockSpec((1, D_MODEL), lambda t, e: (0, 0))
    return pl.pallas_call(
        _moe_kernel,
        grid=(n // tm, N_EXPERTS),
        in_specs=[
            pl.BlockSpec((tm, D_MODEL), lambda t, e: (t, 0)),
            pl.BlockSpec((tm, LANE), lambda t, e: (t, 0)),
            pl.BlockSpec((1, D_MODEL, D_FF), lambda t, e: (e, 0, 0)),
            pl.BlockSpec((1, D_MODEL, D_FF), lambda t, e: (e, 0, 0)),
            pl.BlockSpec((1, D_FF, D_MODEL), lambda t, e: (e, 0, 0)),
            vec, vec,
        ],
        out_specs=pl.BlockSpec((tm, D_MODEL), lambda t, e: (t, 0)),
        out_shape=jax.ShapeDtypeStruct((n, D_MODEL), F32),
        scratch_shapes=[pltpu.VMEM((tm, D_MODEL), F32), pltpu.VMEM((tm, D_MODEL), BF16)],
        compiler_params=pltpu.CompilerParams(dimension_semantics=("parallel", "arbitrary")),
        name="moe",
    )(x1, gates, wg, wu, wd, g, beta)


def _rope_tables(T):
    half = MLA_ROPE // 2
    pos = jnp.arange(T, dtype=F32)
    inv = ROPE_BASE ** (-jnp.arange(half, dtype=F32) / half)
    ang = pos[:, None] * inv[None, :]
    cos, sin = jnp.cos(ang), jnp.sin(ang)
    one = jnp.ones((T, MLA_NOPE), F32)
    z16 = jnp.zeros((T, half), F32)
    z32 = jnp.zeros((T, LANE - MLA_NOPE - MLA_ROPE), F32)
    z64 = jnp.zeros((T, MLA_NOPE), F32)
    rc = jnp.concatenate([one, cos, cos, z32], axis=1)
    rs1 = jnp.concatenate([z64, z16, sin, z32], axis=1)
    rs2 = jnp.concatenate([z64, -sin, z16, z32], axis=1)
    return rc, rs1, rs2


def _pack_w_in(w):
    sizes = (Q_LORA, KV_LORA, MLA_ROPE, DSA_HEADS * DSA_DIM, DSA_DIM, DSA_DIM, IDX_HEADS * IDX_DIM, IDX_DIM, IDX_HEADS)
    offs = np.concatenate([[0], np.cumsum(sizes)])
    qa, kva, kr, dq, dk, dv, iq, ik, iw = [w[:, int(offs[j]):int(offs[j + 1])] for j in range(len(sizes))]
    z = lambda n: jnp.zeros((w.shape[0], n), w.dtype)
    cat = jnp.concatenate([
        qa, kva, dq, iq,
        z(MLA_NOPE), kr, z(LANE - MLA_NOPE - MLA_ROPE),
        dk, dv,
        ik, iw, z(LANE - IDX_DIM - IDX_HEADS),
    ], axis=1)
    assert cat.shape[1] == C_END
    return cat.astype(BF16)


def kernel(x, w_in, q_norm_g, w_q_up, kv_norm_g, w_uk, w_uv, w_o, ln1_g, ln1_b, router_w, router_bias,
           w_gate, w_up, w_down, ln2_g, ln2_b):
    B, T, D = x.shape
    rc, rs1, rs2 = _rope_tables(T)
    rwt = router_w.T
    rwh = rwt.astype(BF16)
    rwl = (rwt - rwh.astype(F32)).astype(BF16)
    rb = router_bias.reshape(N_EXPERTS, 1).astype(F32)
    for l in range(DEPTH):
        wcat = _pack_w_in(w_in[l])
        wq = w_q_up[l].reshape(Q_LORA, MLA_HEADS, MLA_NOPE + MLA_ROPE).transpose(1, 0, 2)
        wq = jnp.pad(wq, ((0, 0), (0, 0), (0, HEAD_PAD - MLA_NOPE - MLA_ROPE))).astype(BF16)
        wk = w_uk[l].reshape(KV_LORA, MLA_HEADS, MLA_NOPE).transpose(1, 0, 2)
        wk = jnp.pad(wk, ((0, 0), (0, 0), (0, HEAD_PAD - MLA_NOPE))).astype(BF16)
        wv = w_uv[l].reshape(KV_LORA, MLA_HEADS, MLA_V).transpose(1, 0, 2).astype(BF16)
        q, k, v, dq, dk, dv, iq, ik, iw = _proj_call(
            x, wcat, q_norm_g[l].reshape(1, Q_LORA), kv_norm_g[l].reshape(1, KV_LORA), wq, wk, wv, rc, rs1, rs2)
        out_a = _mla_call(q, k, v)
        out_b = _dsa_call(dq, iq, iw, dk, dv, ik)
        x1, gates = _out_call(out_a, out_b, x, w_o[l].astype(BF16), ln1_g[l].reshape(1, D), ln1_b[l].reshape(1, D),
                              rwh, rwl, rb)
        y = _moe_call(x1.reshape(B * T, D), gates.reshape(B * T, LANE), w_gate[l].astype(BF16), w_up[l].astype(BF16),
                      w_down[l].astype(BF16), ln2_g[l].reshape(1, D), ln2_b[l].reshape(1, D))
        x = y.reshape(B, T, D)
    return x
```

```python
import jax
import jax.numpy as jnp
import numpy as np
from jax import lax
from jax.experimental import pallas as pl
from jax.experimental.pallas import tpu as pltpu

F32 = jnp.float32
BF16 = jnp.bfloat16
I32 = jnp.int32

D_MODEL = 1024
DEPTH = 2
MLA_HEADS = 8
MLA_NOPE = 64
MLA_ROPE = 32
MLA_V = 64
Q_LORA = 384
KV_LORA = 256
ROPE_BASE = 10000.0
DSA_HEADS = 8
DSA_DIM = 64
IDX_HEADS = 8
IDX_DIM = 32
TOPK_MAX = 256
N_EXPERTS = 16
N_GROUPS = 4
EXPERTS_PER_GROUP = 4
D_FF = 512
ALPHA = (2.0 * DEPTH) ** 0.25
LN_EPS = 1e-5
RMS_EPS = 1e-6
MLA_SCALE = (MLA_NOPE + MLA_ROPE) ** -0.5
SLOPES = tuple(2.0 ** (-8.0 * (h + 1) / DSA_HEADS) for h in range(DSA_HEADS))

LANE = 128
HEAD_PAD = LANE
NEG = -1e30

C_QA = 0
C_KVA = C_QA + Q_LORA
C_DQ = C_KVA + KV_LORA
C_IQ = C_DQ + DSA_HEADS * DSA_DIM
C_KR = C_IQ + IDX_HEADS * IDX_DIM
C_KX = C_KR + LANE
C_END = C_KX + LANE
R_DV = 0
R_IW = DSA_DIM
R_END = LANE
IDX_SCALE = (IDX_DIM * IDX_HEADS) ** -0.5
DSA_SCALE = DSA_DIM ** -0.5

TM_PROJ = 512
TQ_MLA = 256
TQ_DSA = 128
TK_DSA = 256
TM_MOE = 1024

NT_DIMS = (((1,), (1,)), ((), ()))


def _rms(x, g):
    return x * lax.rsqrt(jnp.mean(x * x, axis=-1, keepdims=True) + RMS_EPS) * g


def _layer_norm(y, g, b):
    mu = jnp.mean(y, axis=-1, keepdims=True)
    d = y - mu
    var = jnp.mean(d * d, axis=-1, keepdims=True)
    return d * lax.rsqrt(var + LN_EPS) * g + b


def _proj_kernel(x_ref, wcat_ref, wside_ref, qg_ref, kvg_ref, wq_ref, wk_ref, wv_ref, rc_ref, rs1_ref, rs2_ref,
                 q_ref, k_ref, v_ref, dq_ref, dk_ref, dvt_ref, iq_ref, ik_ref, iwt_ref):
    xb = x_ref[0].astype(BF16)
    proj = jnp.dot(xb, wcat_ref[...], preferred_element_type=F32)
    side = lax.dot_general(wside_ref[...], xb, NT_DIMS, preferred_element_type=F32)
    qn = _rms(proj[:, C_QA:C_QA + Q_LORA], qg_ref[...]).astype(BF16)
    ckv = _rms(proj[:, C_KVA:C_KVA + KV_LORA], kvg_ref[...]).astype(BF16)
    rc, rs1, rs2 = rc_ref[...], rs1_ref[...], rs2_ref[...]

    def rope(t):
        return t * rc + pltpu.roll(t, 16, 1) * rs1 + pltpu.roll(t, LANE - 16, 1) * rs2

    kr = rope(proj[:, C_KR:C_KR + LANE])
    for h in range(MLA_HEADS):
        qh = jnp.dot(qn, wq_ref[h], preferred_element_type=F32)
        q_ref[0, h] = rope(qh).astype(BF16)
        kh = jnp.dot(ckv, wk_ref[h], preferred_element_type=F32) + kr
        k_ref[0, h] = kh.astype(BF16)
        v_ref[0, h] = jnp.dot(ckv, wv_ref[h], preferred_element_type=F32).astype(BF16)
    for h in range(DSA_HEADS):
        dq_ref[0, h] = (proj[:, C_DQ + DSA_DIM * h:C_DQ + DSA_DIM * (h + 1)] * DSA_SCALE).astype(BF16)
    for h in range(IDX_HEADS):
        iq_ref[0, h] = proj[:, C_IQ + IDX_DIM * h:C_IQ + IDX_DIM * (h + 1)].astype(BF16)
    dk_ref[0] = proj[:, C_KX:C_KX + DSA_DIM].astype(BF16)
    ik_ref[0] = proj[:, C_KX + DSA_DIM:C_KX + DSA_DIM + IDX_DIM].astype(BF16)
    for j in range(TM_PROJ // TK_DSA):
        dvt_ref[0, j] = side[R_DV:R_DV + DSA_DIM, TK_DSA * j:TK_DSA * (j + 1)].astype(BF16)
    iwt_ref[0] = side[R_IW:R_IW + IDX_HEADS, :] * IDX_SCALE


def _proj_call(x, wcat, wside, qg, kvg, wq, wk, wv, rc, rs1, rs2):
    B, T, _ = x.shape
    tm = TM_PROJ
    full = lambda shape: pl.BlockSpec(shape, lambda b, t: (0,) * len(shape))
    head_out = lambda d: pl.BlockSpec((1, MLA_HEADS, tm, d), lambda b, t: (b, 0, t, 0))
    tok_out = lambda d: pl.BlockSpec((1, tm, d), lambda b, t: (b, t, 0))
    tab = pl.BlockSpec((tm, LANE), lambda b, t: (t, 0))
    return pl.pallas_call(
        _proj_kernel,
        grid=(B, T // tm),
        in_specs=[
            pl.BlockSpec((1, tm, D_MODEL), lambda b, t: (b, t, 0)),
            full(wcat.shape), full(wside.shape), full(qg.shape), full(kvg.shape), full(wq.shape), full(wk.shape),
            full(wv.shape), tab, tab, tab,
        ],
        out_specs=[head_out(HEAD_PAD), head_out(HEAD_PAD), head_out(MLA_V), head_out(DSA_DIM),
                   tok_out(DSA_DIM),
                   pl.BlockSpec((1, tm // TK_DSA, DSA_DIM, TK_DSA), lambda b, t: (b, t, 0, 0)),
                   head_out(IDX_DIM), tok_out(IDX_DIM),
                   pl.BlockSpec((1, IDX_HEADS, tm), lambda b, t: (b, 0, t))],
        out_shape=[
            jax.ShapeDtypeStruct((B, MLA_HEADS, T, HEAD_PAD), BF16),
            jax.ShapeDtypeStruct((B, MLA_HEADS, T, HEAD_PAD), BF16),
            jax.ShapeDtypeStruct((B, MLA_HEADS, T, MLA_V), BF16),
            jax.ShapeDtypeStruct((B, DSA_HEADS, T, DSA_DIM), BF16),
            jax.ShapeDtypeStruct((B, T, DSA_DIM), BF16),
            jax.ShapeDtypeStruct((B, T // TK_DSA, DSA_DIM, TK_DSA), BF16),
            jax.ShapeDtypeStruct((B, IDX_HEADS, T, IDX_DIM), BF16),
            jax.ShapeDtypeStruct((B, T, IDX_DIM), BF16),
            jax.ShapeDtypeStruct((B, IDX_HEADS, T), F32),
        ],
        compiler_params=pltpu.CompilerParams(dimension_semantics=("parallel", "parallel")),
        name="proj",
    )(x, wcat, wside, qg, kvg, wq, wk, wv, rc, rs1, rs2)


def _mla_kernel(q_ref, k_ref, v_ref, o_ref):
    i = pl.program_id(1)
    tq = TQ_MLA
    row = i * tq + lax.broadcasted_iota(I32, (tq, tq), 0)
    col = lax.broadcasted_iota(I32, (tq, tq), 1)
    for h in range(MLA_HEADS):
        qh = q_ref[0, h]

        def body(j, carry, h=h, qh=qh):
            m, l, acc = carry
            start = pl.multiple_of(j * tq, tq)
            kc = k_ref[0, h, pl.ds(start, tq), :]
            vc = v_ref[0, h, pl.ds(start, tq), :]
            s = lax.dot_general(qh, kc, NT_DIMS, preferred_element_type=F32) * MLA_SCALE
            s = jnp.where(col + j * tq <= row, s, NEG)
            m_new = jnp.maximum(m, jnp.max(s, axis=-1, keepdims=True))
            p = jnp.exp(s - m_new)
            a = jnp.exp(m - m_new)
            l = a * l + jnp.sum(p, axis=-1, keepdims=True)
            acc = a * acc + jnp.dot(p.astype(BF16), vc, preferred_element_type=F32)
            return m_new, l, acc

        init = (jnp.full((tq, 1), NEG, F32), jnp.zeros((tq, 1), F32), jnp.zeros((tq, MLA_V), F32))
        _, l, acc = lax.fori_loop(0, i + 1, body, init)
        o_ref[0, :, MLA_V * h:MLA_V * (h + 1)] = (acc / l).astype(BF16)


def _mla_call(q, k, v):
    B, H, T, _ = q.shape
    tq = TQ_MLA
    return pl.pallas_call(
        _mla_kernel,
        grid=(B, T // tq),
        in_specs=[
            pl.BlockSpec((1, H, tq, HEAD_PAD), lambda b, i: (b, 0, i, 0)),
            pl.BlockSpec((1, H, T, HEAD_PAD), lambda b, i: (b, 0, 0, 0)),
            pl.BlockSpec((1, H, T, MLA_V), lambda b, i: (b, 0, 0, 0)),
        ],
        out_specs=pl.BlockSpec((1, tq, H * MLA_V), lambda b, i: (b, i, 0)),
        out_shape=jax.ShapeDtypeStruct((B, T, H * MLA_V), BF16),
        compiler_params=pltpu.CompilerParams(dimension_semantics=("parallel", "arbitrary")),
        name="mla",
    )(q, k, v)


def _dsa_kernel(dq_ref, iq_ref, iwt_ref, dk_ref, dvt_ref, ik_ref, o_ref, sc_ref, m_ref, l_ref, acc_ref):
    i = pl.program_id(1)
    tq, tk = TQ_DSA, TK_DSA
    T = dk_ref.shape[1]
    nch = lax.shift_right_logical(i, 1) + 1
    cols = DSA_HEADS * tq
    key = lax.broadcasted_iota(I32, (tk, tq), 0)
    t_q = i * tq + lax.broadcasted_iota(I32, (tk, tq), 1)
    t_row = i * tq + lax.broadcasted_iota(I32, (1, tq), 1)

    iq2 = iq_ref[0].reshape(cols, IDX_DIM)
    w8 = iwt_ref[0]

    def score_body(c, carry):
        start = pl.multiple_of(c * tk, tk)
        ikc = ik_ref[0, pl.ds(start, tk), :]
        rel = lax.dot_general(ikc, iq2, NT_DIMS, preferred_element_type=F32)
        sc = jnp.maximum(rel[:, 0:tq], 0.0) * w8[0:1]
        for h in range(1, IDX_HEADS):
            sc = sc + jnp.maximum(rel[:, tq * h:tq * (h + 1)], 0.0) * w8[h:h + 1]
        sc_ref[c] = jnp.where(key + c * tk <= t_q, sc, -jnp.inf)
        return carry

    lax.fori_loop(0, nch, score_body, 0)

    def count(pred):
        def body(c, acc):
            return acc + jnp.where(pred(sc_ref[c], key + c * tk), 1.0, 0.0)

        acc = lax.fori_loop(0, nch, body, jnp.zeros((tk, tq), F32))
        return jnp.sum(acc, axis=0, keepdims=True)

    int_min = jnp.int32(-2 ** 31)

    def key_to_f32(ku):
        ks = ku ^ int_min
        bits = jnp.where(ks >= 0, ks, ks ^ jnp.int32(0x7FFFFFFF))
        return lax.bitcast_convert_type(bits, F32)

    def bis_body(step, carry):
        res, cnt_res = carry
        cand = res | lax.shift_left(jnp.int32(1), 31 - step)
        tau_c = key_to_f32(cand)
        cnt = count(lambda sc, idx: sc >= tau_c)
        ok = cnt >= float(TOPK_MAX)
        return jnp.where(ok, cand, res), jnp.where(ok, cnt, cnt_res)

    res, cnt_ge = lax.fori_loop(0, 32, bis_body,
                                (jnp.zeros((1, tq), I32), jnp.full((1, tq), float(TOPK_MAX), F32)))
    short = t_row < TOPK_MAX - 1
    tau = jnp.where(short, -jnp.inf, key_to_f32(res))
    cnt_ge = jnp.where(short, float(TOPK_MAX), cnt_ge)

    def tie_search():
        need = float(TOPK_MAX) - count(lambda sc, idx: sc > tau)

        def tie_body(step, jm):
            cand = jm | lax.shift_left(jnp.int32(1), 10 - step)
            cnt = count(lambda sc, idx: (sc >= tau) & jnp.logical_not(sc > tau) & (idx < cand))
            return jnp.where(cnt < need, cand, jm)

        return lax.fori_loop(0, 11, tie_body, jnp.zeros((1, tq), I32))

    j_max = lax.cond(jnp.max(cnt_ge) > float(TOPK_MAX), tie_search, lambda: jnp.full((1, tq), T, I32))

    q2 = dq_ref[0].reshape(cols, DSA_DIM)
    m_ref[...] = jnp.full((1, cols), NEG, F32)
    l_ref[...] = jnp.zeros((1, cols), F32)
    acc_ref[...] = jnp.zeros((DSA_DIM, cols), F32)
    pair = 2 * tq

    def att_body(c, carry):
        start = pl.multiple_of(c * tk, tk)
        kc = dk_ref[0, pl.ds(start, tk), :]
        vt = dvt_ref[0, c]
        sc = sc_ref[c]
        idx = key + c * tk
        sel = (sc >= tau) & ((sc > tau) | (idx <= j_max)) & (sc > -jnp.inf)
        base = jnp.where(sel, 0.0, NEG)
        dist = (t_q - idx).astype(F32)
        m_all, l_all, acc_all = m_ref[...], l_ref[...], acc_ref[...]
        m_out, l_out, acc_out = [], [], []
        for hp in range(DSA_HEADS // 2):
            lanes = slice(pair * hp, pair * (hp + 1))
            s = lax.dot_general(kc, q2[lanes], NT_DIMS, preferred_element_type=F32)
            s = jnp.concatenate(
                [s[:, tq * hh:tq * (hh + 1)] - SLOPES[2 * hp + hh] * dist + base for hh in range(2)], axis=1)
            m_prev = m_all[:, lanes]
            m_new = jnp.maximum(m_prev, jnp.max(s, axis=0, keepdims=True))
            p = jnp.exp(s - m_new)
            a = jnp.exp(m_prev - m_new)
            l_out.append(a * l_all[:, lanes] + jnp.sum(p, axis=0, keepdims=True))
            acc_out.append(a * acc_all[:, lanes] + jnp.dot(vt, p.astype(BF16), preferred_element_type=F32))
            m_out.append(m_new)
        m_ref[...] = jnp.concatenate(m_out, axis=1)
        l_ref[...] = jnp.concatenate(l_out, axis=1)
        acc_ref[...] = jnp.concatenate(acc_out, axis=1)
        return carry

    lax.fori_loop(0, nch, att_body, 0)
    ot = acc_ref[...] / l_ref[...]
    zpad = jnp.zeros((tq - DSA_DIM, tq), F32)
    for h in range(DSA_HEADS):
        oh = jnp.concatenate([ot[:, tq * h:tq * (h + 1)], zpad], axis=0).T
        o_ref[0, :, DSA_DIM * h:DSA_DIM * (h + 1)] = oh[:, 0:DSA_DIM].astype(BF16)


def _dsa_call(dq, iq, iwt, dk, dvt, ik):
    B, H, T, _ = dq.shape
    tq, tk = TQ_DSA, TK_DSA
    cols = H * tq
    return pl.pallas_call(
        _dsa_kernel,
        grid=(B, T // tq),
        in_specs=[
            pl.BlockSpec((1, H, tq, DSA_DIM), lambda b, i: (b, 0, i, 0)),
            pl.BlockSpec((1, H, tq, IDX_DIM), lambda b, i: (b, 0, i, 0)),
            pl.BlockSpec((1, IDX_HEADS, tq), lambda b, i: (b, 0, i)),
            pl.BlockSpec((1, T, DSA_DIM), lambda b, i: (b, 0, 0)),
            pl.BlockSpec((1, T // tk, DSA_DIM, tk), lambda b, i: (b, 0, 0, 0)),
            pl.BlockSpec((1, T, IDX_DIM), lambda b, i: (b, 0, 0)),
        ],
        out_specs=pl.BlockSpec((1, tq, H * DSA_DIM), lambda b, i: (b, i, 0)),
        out_shape=jax.ShapeDtypeStruct((B, T, H * DSA_DIM), BF16),
        scratch_shapes=[
            pltpu.VMEM((T // tk, tk, tq), F32),
            pltpu.VMEM((1, cols), F32),
            pltpu.VMEM((1, cols), F32),
            pltpu.VMEM((DSA_DIM, cols), F32),
        ],
        compiler_params=pltpu.CompilerParams(dimension_semantics=("parallel", "arbitrary")),
        name="dsa",
    )(dq, iq, iwt, dk, dvt, ik)


def _route(aff, biased):
    b = [biased[e:e + 1] for e in range(N_EXPERTS)]
    a = [aff[e:e + 1] for e in range(N_EXPERTS)]
    gs = []
    for g in range(N_GROUPS):
        v0, v1, v2, v3 = b[4 * g:4 * g + 4]
        hi1, lo1 = jnp.maximum(v0, v1), jnp.minimum(v0, v1)
        hi2, lo2 = jnp.maximum(v2, v3), jnp.minimum(v2, v3)
        gs.append(jnp.maximum(hi1, hi2) + jnp.maximum(jnp.minimum(hi1, hi2), jnp.maximum(lo1, lo2)))
    best, sel = gs[0], jnp.zeros_like(gs[0], dtype=I32)
    for g in range(1, N_GROUPS):
        upd = gs[g] > best
        sel = jnp.where(upd, g, sel)
        best = jnp.where(upd, gs[g], best)
    v, av = [], []
    for k in range(EXPERTS_PER_GROUP):
        vk, ak = b[k], a[k]
        for g in range(1, N_GROUPS):
            vk = jnp.where(sel == g, b[4 * g + k], vk)
            ak = jnp.where(sel == g, a[4 * g + k], ak)
        v.append(vk)
        av.append(ak)
    m1, i1 = v[0], jnp.zeros_like(sel)
    for k in range(1, EXPERTS_PER_GROUP):
        upd = v[k] > m1
        i1 = jnp.where(upd, k, i1)
        m1 = jnp.where(upd, v[k], m1)
    m2, i2 = jnp.full_like(m1, -jnp.inf), jnp.zeros_like(sel)
    for k in range(EXPERTS_PER_GROUP):
        cand = jnp.where(i1 == k, -jnp.inf, v[k])
        upd = cand > m2
        i2 = jnp.where(upd, k, i2)
        m2 = jnp.where(upd, cand, m2)
    a1, a2 = jnp.zeros_like(m1), jnp.zeros_like(m1)
    for k in range(EXPERTS_PER_GROUP):
        a1 = jnp.where(i1 == k, av[k], a1)
        a2 = jnp.where(i2 == k, av[k], a2)
    tot = a1 + a2
    w1, w2 = a1 / tot, a2 / tot
    gates = []
    for e in range(N_EXPERTS):
        g, k = divmod(e, EXPERTS_PER_GROUP)
        in_g = sel == g
        gates.append(jnp.where(in_g & (i1 == k), w1, 0.0) + jnp.where(in_g & (i2 == k), w2, 0.0))
    return gates


def _out_kernel(a_ref, b_ref, x_ref, wo_ref, g_ref, beta_ref, rwh_ref, rwl_ref, rb_ref, x1_ref, gates_ref):
    half = MLA_HEADS * MLA_V
    mix = jnp.dot(a_ref[0], wo_ref[0:half, :], preferred_element_type=F32)
    mix = mix + jnp.dot(b_ref[0], wo_ref[half:2 * half, :], preferred_element_type=F32)
    x1 = _layer_norm(ALPHA * x_ref[0] + mix, g_ref[...], beta_ref[...])
    x1_ref[0] = x1
    xh = x1.astype(BF16)
    xl = (x1 - xh.astype(F32)).astype(BF16)
    rwh, rwl = rwh_ref[...], rwl_ref[...]
    logits = lax.dot_general(rwh, xh, NT_DIMS, preferred_element_type=F32)
    logits = logits + lax.dot_general(rwh, xl, NT_DIMS, preferred_element_type=F32)
    logits = logits + lax.dot_general(rwl, xh, NT_DIMS, preferred_element_type=F32)
    aff = 1.0 / (1.0 + jnp.exp(-logits))
    gates = _route(aff, aff + rb_ref[...])
    tm = logits.shape[1]
    gt = jnp.concatenate(gates + [jnp.zeros((LANE - N_EXPERTS, tm), F32)], axis=0)
    gates_ref[0] = gt.T


def _out_call(a, b, x, wo, g, beta, rwh, rwl, rb):
    B, T, _ = x.shape
    tm = TM_PROJ
    full = lambda shape: pl.BlockSpec(shape, lambda bb, t: (0,) * len(shape))
    tok = lambda d: pl.BlockSpec((1, tm, d), lambda bb, t: (bb, t, 0))
    return pl.pallas_call(
        _out_kernel,
        grid=(B, T // tm),
        in_specs=[tok(a.shape[-1]), tok(b.shape[-1]), tok(D_MODEL), full(wo.shape), full(g.shape), full(beta.shape),
                  full(rwh.shape), full(rwl.shape), full(rb.shape)],
        out_specs=[tok(D_MODEL), tok(LANE)],
        out_shape=[jax.ShapeDtypeStruct((B, T, D_MODEL), F32), jax.ShapeDtypeStruct((B, T, LANE), F32)],
        compiler_params=pltpu.CompilerParams(dimension_semantics=("parallel", "parallel")),
        name="outproj",
    )(a, b, x, wo, g, beta, rwh, rwl, rb)


def _moe_kernel(x_ref, gates_ref, wg_ref, wu_ref, wd_ref, g_ref, beta_ref, o_ref, acc_ref, xb_ref):
    e = pl.program_id(1)

    @pl.when(e == 0)
    def _():
        acc_ref[...] = jnp.zeros_like(acc_ref)
        xb_ref[...] = x_ref[...].astype(BF16)

    xb = xb_ref[...]
    hg = jnp.dot(xb, wg_ref[0], preferred_element_type=F32)
    hu = jnp.dot(xb, wu_ref[0], preferred_element_type=F32)
    gates = gates_ref[...]
    lane = lax.broadcasted_iota(I32, gates.shape, 1)
    gate = jnp.sum(jnp.where(lane == e, gates, 0.0), axis=-1, keepdims=True)
    act = (hg / (1.0 + jnp.exp(-hg))) * hu * gate
    acc_ref[...] += jnp.dot(act.astype(BF16), wd_ref[0], preferred_element_type=F32)

    @pl.when(e == N_EXPERTS - 1)
    def _():
        o_ref[...] = _layer_norm(ALPHA * x_ref[...] + acc_ref[...], g_ref[...], beta_ref[...])


def _moe_call(x1, gates, wg, wu, wd, g, beta):
    n = x1.shape[0]
    tm = TM_MOE
    vec = pl.BlockSpec((1, D_MODEL), lambda t, e: (0, 0))
    return pl.pallas_call(
        _moe_kernel,
        grid=(n // tm, N_EXPERTS),
        in_specs=[
            pl.BlockSpec((tm, D_MODEL), lambda t, e: (t, 0)),
            pl.BlockSpec((tm, LANE), lambda t, e: (t, 0)),
            pl.BlockSpec((1, D_MODEL, D_FF), lambda t, e: (e, 0, 0)),
            pl.BlockSpec((1, D_MODEL, D_FF), lambda t, e: (e, 0, 0)),
            pl.BlockSpec((1, D_FF, D_MODEL), lambda t, e: (e, 0, 0)),
            vec, vec,
        ],
        out_specs=pl.BlockSpec((tm, D_MODEL), lambda t, e: (t, 0)),
        out_shape=jax.ShapeDtypeStruct((n, D_MODEL), F32),
        scratch_shapes=[pltpu.VMEM((tm, D_MODEL), F32), pltpu.VMEM((tm, D_MODEL), BF16)],
        compiler_params=pltpu.CompilerParams(dimension_semantics=("parallel", "arbitrary")),
        name="moe",
    )(x1, gates, wg, wu, wd, g, beta)


def _rope_tables(T):
    half = MLA_ROPE // 2
    pos = jnp.arange(T, dtype=F32)
    inv = ROPE_BASE ** (-jnp.arange(half, dtype=F32) / half)
    ang = pos[:, None] * inv[None, :]
    cos, sin = jnp.cos(ang), jnp.sin(ang)
    one = jnp.ones((T, MLA_NOPE), F32)
    z16 = jnp.zeros((T, half), F32)
    z32 = jnp.zeros((T, LANE - MLA_NOPE - MLA_ROPE), F32)
    z64 = jnp.zeros((T, MLA_NOPE), F32)
    rc = jnp.concatenate([one, cos, cos, z32], axis=1)
    rs1 = jnp.concatenate([z64, z16, sin, z32], axis=1)
    rs2 = jnp.concatenate([z64, -sin, z16, z32], axis=1)
    return rc, rs1, rs2


def _pack_w_in(w):
    sizes = (Q_LORA, KV_LORA, MLA_ROPE, DSA_HEADS * DSA_DIM, DSA_DIM, DSA_DIM, IDX_HEADS * IDX_DIM, IDX_DIM, IDX_HEADS)
    offs = np.concatenate([[0], np.cumsum(sizes)])
    qa, kva, kr, dq, dk, dv, iq, ik, iw = [w[:, int(offs[j]):int(offs[j + 1])] for j in range(len(sizes))]
    z = lambda n: jnp.zeros((w.shape[0], n), w.dtype)
    cat = jnp.concatenate([
        qa, kva, dq, iq,
        z(MLA_NOPE), kr, z(LANE - MLA_NOPE - MLA_ROPE),
        dk, ik, z(LANE - DSA_DIM - IDX_DIM),
    ], axis=1)
    assert cat.shape[1] == C_END
    side = jnp.concatenate([dv, iw, z(R_END - DSA_DIM - IDX_HEADS)], axis=1).T
    return cat.astype(BF16), side.astype(BF16)


def kernel(x, w_in, q_norm_g, w_q_up, kv_norm_g, w_uk, w_uv, w_o, ln1_g, ln1_b, router_w, router_bias,
           w_gate, w_up, w_down, ln2_g, ln2_b):
    B, T, D = x.shape
    rc, rs1, rs2 = _rope_tables(T)
    rwt = router_w.T
    rwh = rwt.astype(BF16)
    rwl = (rwt - rwh.astype(F32)).astype(BF16)
    rb = router_bias.reshape(N_EXPERTS, 1).astype(F32)
    for l in range(DEPTH):
        wcat, wside = _pack_w_in(w_in[l])
        wq = w_q_up[l].reshape(Q_LORA, MLA_HEADS, MLA_NOPE + MLA_ROPE).transpose(1, 0, 2)
        wq = jnp.pad(wq, ((0, 0), (0, 0), (0, HEAD_PAD - MLA_NOPE - MLA_ROPE))).astype(BF16)
        wk = w_uk[l].reshape(KV_LORA, MLA_HEADS, MLA_NOPE).transpose(1, 0, 2)
        wk = jnp.pad(wk, ((0, 0), (0, 0), (0, HEAD_PAD - MLA_NOPE))).astype(BF16)
        wv = w_uv[l].reshape(KV_LORA, MLA_HEADS, MLA_V).transpose(1, 0, 2).astype(BF16)
        q, k, v, dq, dk, dvt, iq, ik, iwt = _proj_call(
            x, wcat, wside, q_norm_g[l].reshape(1, Q_LORA), kv_norm_g[l].reshape(1, KV_LORA), wq, wk, wv,
            rc, rs1, rs2)
        out_a = _mla_call(q, k, v)
        out_b = _dsa_call(dq, iq, iwt, dk, dvt, ik)
        x1, gates = _out_call(out_a, out_b, x, w_o[l].astype(BF16), ln1_g[l].reshape(1, D), ln1_b[l].reshape(1, D),
                              rwh, rwl, rb)
        y = _moe_call(x1.reshape(B * T, D), gates.reshape(B * T, LANE), w_gate[l].astype(BF16), w_up[l].astype(BF16),
                      w_down[l].astype(BF16), ln2_g[l].reshape(1, D), ln2_b[l].reshape(1, D))
        x = y.reshape(B, T, D)
    return x
```

```python
import jax
import jax.numpy as jnp
import numpy as np
from jax import lax
from jax.experimental import pallas as pl
from jax.experimental.pallas import tpu as pltpu

F32 = jnp.float32
BF16 = jnp.bfloat16
I32 = jnp.int32

D_MODEL = 1024
DEPTH = 2
MLA_HEADS = 8
MLA_NOPE = 64
MLA_ROPE = 32
MLA_V = 64
Q_LORA = 384
KV_LORA = 256
ROPE_BASE = 10000.0
DSA_HEADS = 8
DSA_DIM = 64
IDX_HEADS = 8
IDX_DIM = 32
TOPK_MAX = 256
N_EXPERTS = 16
N_GROUPS = 4
EXPERTS_PER_GROUP = 4
D_FF = 512
ALPHA = (2.0 * DEPTH) ** 0.25
LN_EPS = 1e-5
RMS_EPS = 1e-6
MLA_SCALE = (MLA_NOPE + MLA_ROPE) ** -0.5
LOG2E = 1.4426950408889634
SLOPES = tuple(2.0 ** (-8.0 * (h + 1) / DSA_HEADS) for h in range(DSA_HEADS))

LANE = 128
HEAD_PAD = LANE
NEG = -1e30

C_QA = 0
C_KVA = C_QA + Q_LORA
C_DQ = C_KVA + KV_LORA
C_IQ = C_DQ + DSA_HEADS * DSA_DIM
C_KR = C_IQ + IDX_HEADS * IDX_DIM
C_KX = C_KR + LANE
C_END = C_KX + LANE
R_DV = 0
R_IW = DSA_DIM
R_END = LANE
IDX_SCALE = (IDX_DIM * IDX_HEADS) ** -0.5
DSA_SCALE = DSA_DIM ** -0.5

TM_PROJ = 512
TQ_MLA = 256
TQ_DSA = 128
TK_DSA = 256
TOP_K = 2
TR_MOE = 512
TM_DISP = 512

NT_DIMS = (((1,), (1,)), ((), ()))


def _rms(x, g):
    return x * lax.rsqrt(jnp.mean(x * x, axis=-1, keepdims=True) + RMS_EPS) * g


def _layer_norm(y, g, b):
    mu = jnp.mean(y, axis=-1, keepdims=True)
    d = y - mu
    var = jnp.mean(d * d, axis=-1, keepdims=True)
    return d * lax.rsqrt(var + LN_EPS) * g + b


def _proj_kernel(x_ref, wcat_ref, wside_ref, qg_ref, kvg_ref, wq_ref, wk_ref, wv_ref, rc_ref, rs1_ref, rs2_ref,
                 qfeat_ref, kfeat_ref,
                 q_ref, k_ref, vt_ref, dq_ref, dk_ref, dvt_ref, iq_ref, ik_ref, iwt_ref):
    xb = x_ref[0].astype(BF16)
    proj = jnp.dot(xb, wcat_ref[...], preferred_element_type=F32)
    side = lax.dot_general(wside_ref[...], xb, NT_DIMS, preferred_element_type=F32)
    qn = _rms(proj[:, C_QA:C_QA + Q_LORA], qg_ref[...]).astype(BF16)
    ckv = _rms(proj[:, C_KVA:C_KVA + KV_LORA], kvg_ref[...]).astype(BF16)
    rc, rs1, rs2 = rc_ref[...], rs1_ref[...], rs2_ref[...]

    def rope(t):
        return t * rc + pltpu.roll(t, 16, 1) * rs1 + pltpu.roll(t, LANE - 16, 1) * rs2

    kr = rope(proj[:, C_KR:C_KR + LANE])
    tm = xb.shape[0]
    for h in range(MLA_HEADS):
        qh = jnp.dot(qn, wq_ref[h], preferred_element_type=F32)
        q_ref[0, h] = (rope(qh) * (MLA_SCALE * LOG2E)).astype(BF16)
        kh = jnp.dot(ckv, wk_ref[h], preferred_element_type=F32) + kr
        k_ref[0, h] = kh.astype(BF16)
        vth = lax.dot_general(wv_ref[h], ckv, NT_DIMS, preferred_element_type=F32)
        for j in range(tm // TQ_MLA):
            vt_ref[0, h, j] = vth[:, TQ_MLA * j:TQ_MLA * (j + 1)].astype(BF16)
    for h in range(DSA_HEADS):
        dqh = proj[:, C_DQ + DSA_DIM * h:C_DQ + DSA_DIM * (h + 1)] * (DSA_SCALE * LOG2E)
        feat = jnp.broadcast_to(qfeat_ref[h:h + 1, :], (tm, LANE - DSA_DIM))
        dq_ref[0, h] = jnp.concatenate([dqh, feat], axis=1).astype(BF16)
    for h in range(IDX_HEADS):
        iq_ref[0, h] = proj[:, C_IQ + IDX_DIM * h:C_IQ + IDX_DIM * (h + 1)].astype(BF16)
    dk_ref[0] = jnp.concatenate([proj[:, C_KX:C_KX + DSA_DIM], kfeat_ref[...]], axis=1).astype(BF16)
    ik_ref[0] = proj[:, C_KX + DSA_DIM:C_KX + DSA_DIM + IDX_DIM].astype(BF16)
    for j in range(TM_PROJ // TK_DSA):
        dvt_ref[0, j] = side[R_DV:R_DV + DSA_DIM, TK_DSA * j:TK_DSA * (j + 1)].astype(BF16)
    iwt_ref[0] = side[R_IW:R_IW + IDX_HEADS, :] * IDX_SCALE


def _proj_call(x, wcat, wside, qg, kvg, wq, wk, wv, rc, rs1, rs2, qfeat, kfeat):
    B, T, _ = x.shape
    tm = TM_PROJ
    full = lambda shape: pl.BlockSpec(shape, lambda b, t: (0,) * len(shape))
    head_out = lambda d: pl.BlockSpec((1, MLA_HEADS, tm, d), lambda b, t: (b, 0, t, 0))
    tok_out = lambda d: pl.BlockSpec((1, tm, d), lambda b, t: (b, t, 0))
    tab = pl.BlockSpec((tm, LANE), lambda b, t: (t, 0))
    return pl.pallas_call(
        _proj_kernel,
        grid=(B, T // tm),
        in_specs=[
            pl.BlockSpec((1, tm, D_MODEL), lambda b, t: (b, t, 0)),
            full(wcat.shape), full(wside.shape), full(qg.shape), full(kvg.shape), full(wq.shape), full(wk.shape),
            full(wv.shape), tab, tab, tab, full(qfeat.shape),
            pl.BlockSpec((tm, LANE - DSA_DIM), lambda b, t: (t, 0)),
        ],
        out_specs=[head_out(HEAD_PAD), head_out(HEAD_PAD),
                   pl.BlockSpec((1, MLA_HEADS, tm // TQ_MLA, MLA_V, TQ_MLA), lambda b, t: (b, 0, t, 0, 0)),
                   head_out(LANE),
                   tok_out(LANE),
                   pl.BlockSpec((1, tm // TK_DSA, DSA_DIM, TK_DSA), lambda b, t: (b, t, 0, 0)),
                   head_out(IDX_DIM), tok_out(IDX_DIM),
                   pl.BlockSpec((1, IDX_HEADS, tm), lambda b, t: (b, 0, t))],
        out_shape=[
            jax.ShapeDtypeStruct((B, MLA_HEADS, T, HEAD_PAD), BF16),
            jax.ShapeDtypeStruct((B, MLA_HEADS, T, HEAD_PAD), BF16),
            jax.ShapeDtypeStruct((B, MLA_HEADS, T // TQ_MLA, MLA_V, TQ_MLA), BF16),
            jax.ShapeDtypeStruct((B, DSA_HEADS, T, LANE), BF16),
            jax.ShapeDtypeStruct((B, T, LANE), BF16),
            jax.ShapeDtypeStruct((B, T // TK_DSA, DSA_DIM, TK_DSA), BF16),
            jax.ShapeDtypeStruct((B, IDX_HEADS, T, IDX_DIM), BF16),
            jax.ShapeDtypeStruct((B, T, IDX_DIM), BF16),
            jax.ShapeDtypeStruct((B, IDX_HEADS, T), F32),
        ],
        compiler_params=pltpu.CompilerParams(dimension_semantics=("parallel", "parallel")),
        name="proj",
    )(x, wcat, wside, qg, kvg, wq, wk, wv, rc, rs1, rs2, qfeat, kfeat)


def _transposed_heads_store(o_ref, ot, heads, width, tq):
    zpad = jnp.zeros((LANE - width, tq), F32)
    for h in range(heads):
        oh = jnp.concatenate([ot[:, tq * h:tq * (h + 1)], zpad], axis=0).T
        o_ref[0, :, width * h:width * (h + 1)] = oh[:, 0:width].astype(BF16)


def _mla_kernel(q_ref, k_ref, vt_ref, o_ref, m_ref, l_ref, acc_ref, base_ref):
    i = pl.program_id(1)
    tq = TQ_MLA
    heads = MLA_HEADS
    m_ref[...] = jnp.full(m_ref.shape, NEG, F32)
    l_ref[...] = jnp.zeros(l_ref.shape, F32)
    acc_ref[...] = jnp.zeros(acc_ref.shape, F32)
    key = lax.broadcasted_iota(I32, (tq, tq), 0)
    qpos = lax.broadcasted_iota(I32, (tq, tq), 1)
    base_ref[...] = jnp.where(key <= qpos, -NEG, NEG)

    def step(j, masked):
        start = pl.multiple_of(j * tq, tq)
        s_list = [lax.dot_general(k_ref[0, h, pl.ds(start, tq), :], q_ref[0, h], NT_DIMS,
                                  preferred_element_type=F32) for h in range(heads)]
        m_all, l_all, acc_all = m_ref[...], l_ref[...], acc_ref[...]
        m_out, l_out, a_out, pv_out = [], [], [], []
        for h in range(heads):
            lanes = slice(tq * h, tq * (h + 1))
            s = jnp.minimum(s_list[h], base_ref[...]) if masked else s_list[h]
            m_prev = m_all[:, lanes]
            m_new = jnp.maximum(m_prev, jnp.max(s, axis=0, keepdims=True))
            p = jnp.exp2(s - m_new)
            a = jnp.exp2(m_prev - m_new)
            l_out.append(a * l_all[:, lanes] + jnp.sum(p, axis=0, keepdims=True))
            pv_out.append(jnp.dot(vt_ref[0, h, j], p.astype(BF16), preferred_element_type=F32))
            a_out.append(a)
            m_out.append(m_new)
        m_ref[...] = jnp.concatenate(m_out, axis=1)
        l_ref[...] = jnp.concatenate(l_out, axis=1)
        acc_ref[...] = jnp.concatenate(a_out, axis=1) * acc_all + jnp.concatenate(pv_out, axis=1)

    def body(j, carry):
        step(j, False)
        return carry

    def body_diag(j, carry):
        step(j, True)
        return carry

    lax.fori_loop(0, i, body, 0)
    lax.fori_loop(i, i + 1, body_diag, 0)
    _transposed_heads_store(o_ref, acc_ref[...] / l_ref[...], heads, MLA_V, tq)


def _mla_call(q, k, vt):
    B, H, T, _ = q.shape
    tq = TQ_MLA
    return pl.pallas_call(
        _mla_kernel,
        grid=(B, T // tq),
        in_specs=[
            pl.BlockSpec((1, H, tq, HEAD_PAD), lambda b, i: (b, 0, i, 0)),
            pl.BlockSpec((1, H, T, HEAD_PAD), lambda b, i: (b, 0, 0, 0)),
            pl.BlockSpec((1, H, T // tq, MLA_V, tq), lambda b, i: (b, 0, 0, 0, 0)),
        ],
        out_specs=pl.BlockSpec((1, tq, H * MLA_V), lambda b, i: (b, i, 0)),
        out_shape=jax.ShapeDtypeStruct((B, T, H * MLA_V), BF16),
        scratch_shapes=[
            pltpu.VMEM((1, H * tq), F32),
            pltpu.VMEM((1, H * tq), F32),
            pltpu.VMEM((MLA_V, H * tq), F32),
            pltpu.VMEM((tq, tq), F32),
        ],
        compiler_params=pltpu.CompilerParams(dimension_semantics=("parallel", "arbitrary")),
        name="mla",
    )(q, k, vt)


def _dsa_kernel(dq_ref, iq_ref, iwt_ref, dk_ref, dvt_ref, ik_ref, o_ref, sc_ref, m_ref, l_ref, acc_ref):
    i = pl.program_id(1)
    tq, tk = TQ_DSA, TK_DSA
    T = dk_ref.shape[1]
    nch = lax.shift_right_logical(i, 1) + 1
    cols = DSA_HEADS * tq
    key = lax.broadcasted_iota(I32, (tk, tq), 0)
    t_q = i * tq + lax.broadcasted_iota(I32, (tk, tq), 1)
    t_row = i * tq + lax.broadcasted_iota(I32, (1, tq), 1)

    iq2 = iq_ref[0].reshape(cols, IDX_DIM)
    w8 = iwt_ref[0]

    def score_body(c, carry):
        start = pl.multiple_of(c * tk, tk)
        ikc = ik_ref[0, pl.ds(start, tk), :]
        rel = lax.dot_general(ikc, iq2, NT_DIMS, preferred_element_type=F32)
        sc = jnp.maximum(rel[:, 0:tq], 0.0) * w8[0:1]
        for h in range(1, IDX_HEADS):
            sc = sc + jnp.maximum(rel[:, tq * h:tq * (h + 1)], 0.0) * w8[h:h + 1]
        sc_ref[c] = jnp.where(key + c * tk <= t_q, sc, -jnp.inf)
        return carry

    lax.fori_loop(0, nch, score_body, 0)

    def count(pred):
        def body(c, acc):
            return acc + jnp.where(pred(sc_ref[c], key + c * tk), 1.0, 0.0)

        acc = lax.fori_loop(0, nch, body, jnp.zeros((tk, tq), F32))
        return jnp.sum(acc, axis=0, keepdims=True)

    int_min = jnp.int32(-2 ** 31)

    def key_to_f32(ku):
        ks = ku ^ int_min
        bits = jnp.where(ks >= 0, ks, ks ^ jnp.int32(0x7FFFFFFF))
        return lax.bitcast_convert_type(bits, F32)

    def bis_body(step, carry):
        res, cnt_res = carry
        cand = res | lax.shift_left(jnp.int32(1), 31 - step)
        tau_c = key_to_f32(cand)
        cnt = count(lambda sc, idx: sc >= tau_c)
        ok = cnt >= float(TOPK_MAX)
        return jnp.where(ok, cand, res), jnp.where(ok, cnt, cnt_res)

    res, cnt_ge = lax.fori_loop(0, 32, bis_body,
                                (jnp.zeros((1, tq), I32), jnp.full((1, tq), float(TOPK_MAX), F32)))
    short = t_row < TOPK_MAX - 1
    tau = jnp.where(short, -jnp.inf, key_to_f32(res))
    cnt_ge = jnp.where(short, float(TOPK_MAX), cnt_ge)

    def tie_search():
        need = float(TOPK_MAX) - count(lambda sc, idx: sc > tau)

        def tie_body(step, jm):
            cand = jm | lax.shift_left(jnp.int32(1), 10 - step)
            cnt = count(lambda sc, idx: (sc >= tau) & jnp.logical_not(sc > tau) & (idx < cand))
            return jnp.where(cnt < need, cand, jm)

        return lax.fori_loop(0, 11, tie_body, jnp.zeros((1, tq), I32))

    j_max = lax.cond(jnp.max(cnt_ge) > float(TOPK_MAX), tie_search, lambda: jnp.full((1, tq), T, I32))

    q2 = dq_ref[0].reshape(cols, LANE)
    m_ref[...] = jnp.full((1, cols), NEG, F32)
    l_ref[...] = jnp.zeros((1, cols), F32)
    acc_ref[...] = jnp.zeros((DSA_DIM, cols), F32)

    def att_body(c, carry):
        start = pl.multiple_of(c * tk, tk)
        kc = dk_ref[0, pl.ds(start, tk), :]
        vt = dvt_ref[0, c]
        sc = sc_ref[c]
        idx = key + c * tk
        sel = (sc >= tau) & ((sc > tau) | (idx <= j_max)) & (sc > -jnp.inf)
        cap = jnp.where(sel, -NEG, NEG)
        m_all, l_all, acc_all = m_ref[...], l_ref[...], acc_ref[...]
        m_out, l_out, a_out, p_out = [], [], [], []
        s_all = lax.dot_general(kc, q2, NT_DIMS, preferred_element_type=F32)
        for h in range(DSA_HEADS):
            lanes = slice(tq * h, tq * (h + 1))
            s = jnp.minimum(s_all[:, lanes], cap)
            m_prev = m_all[:, lanes]
            m_new = jnp.maximum(m_prev, jnp.max(s, axis=0, keepdims=True))
            p = jnp.exp2(s - m_new)
            a = jnp.exp2(m_prev - m_new)
            l_out.append(a * l_all[:, lanes] + jnp.sum(p, axis=0, keepdims=True))
            p_out.append(p.astype(BF16))
            a_out.append(a)
            m_out.append(m_new)
        pv = jnp.dot(vt, jnp.concatenate(p_out, axis=1), preferred_element_type=F32)
        m_ref[...] = jnp.concatenate(m_out, axis=1)
        l_ref[...] = jnp.concatenate(l_out, axis=1)
        acc_ref[...] = jnp.concatenate(a_out, axis=1) * acc_all + pv
        return carry

    lax.fori_loop(0, nch, att_body, 0)
    _transposed_heads_store(o_ref, acc_ref[...] / l_ref[...], DSA_HEADS, DSA_DIM, tq)


def _dsa_call(dq, iq, iwt, dk, dvt, ik):
    B, H, T, _ = dq.shape
    tq, tk = TQ_DSA, TK_DSA
    cols = H * tq
    return pl.pallas_call(
        _dsa_kernel,
        grid=(B, T // tq),
        in_specs=[
            pl.BlockSpec((1, H, tq, LANE), lambda b, i: (b, 0, i, 0)),
            pl.BlockSpec((1, H, tq, IDX_DIM), lambda b, i: (b, 0, i, 0)),
            pl.BlockSpec((1, IDX_HEADS, tq), lambda b, i: (b, 0, i)),
            pl.BlockSpec((1, T, LANE), lambda b, i: (b, 0, 0)),
            pl.BlockSpec((1, T // tk, DSA_DIM, tk), lambda b, i: (b, 0, 0, 0)),
            pl.BlockSpec((1, T, IDX_DIM), lambda b, i: (b, 0, 0)),
        ],
        out_specs=pl.BlockSpec((1, tq, H * DSA_DIM), lambda b, i: (b, i, 0)),
        out_shape=jax.ShapeDtypeStruct((B, T, H * DSA_DIM), BF16),
        scratch_shapes=[
            pltpu.VMEM((T // tk, tk, tq), F32),
            pltpu.VMEM((1, cols), F32),
            pltpu.VMEM((1, cols), F32),
            pltpu.VMEM((DSA_DIM, cols), F32),
        ],
        compiler_params=pltpu.CompilerParams(dimension_semantics=("parallel", "arbitrary")),
        name="dsa",
    )(dq, iq, iwt, dk, dvt, ik)


def _route(aff, biased):
    b = [biased[e:e + 1] for e in range(N_EXPERTS)]
    a = [aff[e:e + 1] for e in range(N_EXPERTS)]
    gs = []
    for g in range(N_GROUPS):
        v0, v1, v2, v3 = b[4 * g:4 * g + 4]
        hi1, lo1 = jnp.maximum(v0, v1), jnp.minimum(v0, v1)
        hi2, lo2 = jnp.maximum(v2, v3), jnp.minimum(v2, v3)
        gs.append(jnp.maximum(hi1, hi2) + jnp.maximum(jnp.minimum(hi1, hi2), jnp.maximum(lo1, lo2)))
    best, sel = gs[0], jnp.zeros_like(gs[0], dtype=I32)
    for g in range(1, N_GROUPS):
        upd = gs[g] > best
        sel = jnp.where(upd, g, sel)
        best = jnp.where(upd, gs[g], best)
    v, av = [], []
    for k in range(EXPERTS_PER_GROUP):
        vk, ak = b[k], a[k]
        for g in range(1, N_GROUPS):
            vk = jnp.where(sel == g, b[4 * g + k], vk)
            ak = jnp.where(sel == g, a[4 * g + k], ak)
        v.append(vk)
        av.append(ak)
    m1, i1 = v[0], jnp.zeros_like(sel)
    for k in range(1, EXPERTS_PER_GROUP):
        upd = v[k] > m1
        i1 = jnp.where(upd, k, i1)
        m1 = jnp.where(upd, v[k], m1)
    m2, i2 = jnp.full_like(m1, -jnp.inf), jnp.zeros_like(sel)
    for k in range(EXPERTS_PER_GROUP):
        cand = jnp.where(i1 == k, -jnp.inf, v[k])
        upd = cand > m2
        i2 = jnp.where(upd, k, i2)
        m2 = jnp.where(upd, cand, m2)
    a1, a2 = jnp.zeros_like(m1), jnp.zeros_like(m1)
    for k in range(EXPERTS_PER_GROUP):
        a1 = jnp.where(i1 == k, av[k], a1)
        a2 = jnp.where(i2 == k, av[k], a2)
    tot = a1 + a2
    e1 = sel * EXPERTS_PER_GROUP + i1
    e2 = sel * EXPERTS_PER_GROUP + i2
    return e1, e2, a1 / tot, a2 / tot


def _out_kernel(a_ref, b_ref, x_ref, wo_ref, g_ref, beta_ref, rwh_ref, rwl_ref, rb_ref, x1_ref, route_ref):
    half = MLA_HEADS * MLA_V
    mix = jnp.dot(a_ref[0], wo_ref[0:half, :], preferred_element_type=F32)
    mix = mix + jnp.dot(b_ref[0], wo_ref[half:2 * half, :], preferred_element_type=F32)
    x1 = _layer_norm(ALPHA * x_ref[0] + mix, g_ref[...], beta_ref[...])
    x1_ref[0] = x1
    xh = x1.astype(BF16)
    xl = (x1 - xh.astype(F32)).astype(BF16)
    rwh, rwl = rwh_ref[...], rwl_ref[...]
    logits = lax.dot_general(rwh, xh, NT_DIMS, preferred_element_type=F32)
    logits = logits + lax.dot_general(rwh, xl, NT_DIMS, preferred_element_type=F32)
    logits = logits + lax.dot_general(rwl, xh, NT_DIMS, preferred_element_type=F32)
    aff = 1.0 / (1.0 + jnp.exp(-logits))
    e1, e2, w1, w2 = _route(aff, aff + rb_ref[...])
    tm = logits.shape[1]
    rows = [e1.astype(F32), e2.astype(F32), w1, w2, jnp.zeros((LANE - 4, tm), F32)]
    route_ref[0] = jnp.concatenate(rows, axis=0).T


def _out_call(a, b, x, wo, g, beta, rwh, rwl, rb):
    B, T, _ = x.shape
    tm = TM_PROJ
    full = lambda shape: pl.BlockSpec(shape, lambda bb, t: (0,) * len(shape))
    tok = lambda d: pl.BlockSpec((1, tm, d), lambda bb, t: (bb, t, 0))
    return pl.pallas_call(
        _out_kernel,
        grid=(B, T // tm),
        in_specs=[tok(a.shape[-1]), tok(b.shape[-1]), tok(D_MODEL), full(wo.shape), full(g.shape), full(beta.shape),
                  full(rwh.shape), full(rwl.shape), full(rb.shape)],
        out_specs=[tok(D_MODEL), tok(LANE)],
        out_shape=[jax.ShapeDtypeStruct((B, T, D_MODEL), F32), jax.ShapeDtypeStruct((B, T, LANE), F32)],
        compiler_params=pltpu.CompilerParams(dimension_semantics=("parallel", "parallel")),
        name="outproj",
    )(a, b, x, wo, g, beta, rwh, rwl, rb)


def _route_positions(route, n_tiles):
    eid = route[:, 0:TOP_K].astype(I32).reshape(-1)
    onehot = (eid[:, None] == jnp.arange(N_EXPERTS, dtype=I32)[None, :]).astype(I32)
    csum = jnp.cumsum(onehot, axis=0)
    rank = jnp.sum(onehot * csum, axis=1) - 1
    counts = csum[-1]
    padded = ((counts + TR_MOE - 1) // TR_MOE) * TR_MOE
    ends = jnp.cumsum(padded)
    starts = ends - padded
    pos = jnp.sum(onehot * starts[None, :], axis=1) + rank
    n_active = ends[-1] // TR_MOE
    tile_row = jnp.minimum(jnp.arange(n_tiles, dtype=I32), n_active - 1) * TR_MOE
    tile_expert = jnp.sum((tile_row[:, None] >= ends[None, :]).astype(I32), axis=1)
    return pos.astype(I32), tile_expert.astype(I32), n_active.reshape(1).astype(I32)


SUBLANES = 8


def _row_copy_loop(tm, copy):
    def body(q, carry):
        base = pl.multiple_of(q * SUBLANES, SUBLANES)
        for j in range(SUBLANES):
            for k in range(TOP_K):
                copy(base + j, (base + j) * TOP_K + k, k).start(priority=k)
        return carry

    lax.fori_loop(0, tm // SUBLANES, body, 0)


def _dispatch_kernel(pos_ref, x_ref, xs_in_ref, xs_ref, xr_ref, sem):
    del xs_in_ref
    tm = x_ref.shape[0]
    xr_ref[...] = x_ref[...].reshape(tm, SUBLANES, LANE)
    _row_copy_loop(tm, lambda r, p, k: pltpu.make_async_copy(xr_ref.at[r], xs_ref.at[pos_ref[p]], sem))
    for k in range(TOP_K):
        pltpu.make_async_copy(xr_ref, xs_ref.at[pl.ds(0, tm)], sem).wait()


def _dispatch_call(pos, x1, xs_zero):
    n = x1.shape[0]
    tm = TM_DISP
    return pl.pallas_call(
        _dispatch_kernel,
        grid=(n // tm,),
        in_specs=[
            pl.BlockSpec((TOP_K * tm,), lambda i: (i,), memory_space=pltpu.SMEM),
            pl.BlockSpec((tm, D_MODEL), lambda i: (i, 0)),
            pl.BlockSpec(memory_space=pl.ANY),
        ],
        out_specs=pl.BlockSpec(memory_space=pl.ANY),
        out_shape=jax.ShapeDtypeStruct(xs_zero.shape, xs_zero.dtype),
        scratch_shapes=[pltpu.VMEM((tm, SUBLANES, LANE), F32), pltpu.SemaphoreType.DMA(())],
        input_output_aliases={2: 0},
        compiler_params=pltpu.CompilerParams(dimension_semantics=("arbitrary",)),
        name="dispatch",
    )(pos, x1, xs_zero)


def _expert_kernel(te_ref, na_ref, xs_ref, wg_ref, wu_ref, wd_ref, ys_ref):
    del te_ref
    active = pl.program_id(0) < na_ref[0]

    @pl.when(active)
    def _():
        tr = xs_ref.shape[0]
        xb = xs_ref[...].reshape(tr, D_MODEL).astype(BF16)
        hg = jnp.dot(xb, wg_ref[0], preferred_element_type=F32)
        hu = jnp.dot(xb, wu_ref[0], preferred_element_type=F32)
        act = (hg / (1.0 + jnp.exp(-hg))) * hu
        y = jnp.dot(act.astype(BF16), wd_ref[0], preferred_element_type=F32)
        ys_ref[...] = y.reshape(tr, SUBLANES, LANE)

    @pl.when(jnp.logical_not(active))
    def _():
        ys_ref[...] = jnp.zeros_like(ys_ref)


def _expert_call(tile_expert, n_active, xs, wg, wu, wd):
    rows = xs.shape[0]
    tr = TR_MOE
    row_map = lambda i, te, na: (jnp.minimum(i, na[0] - 1), 0, 0)
    out_map = lambda i, te, na: (i, 0, 0)
    w_map = lambda i, te, na: (te[i], 0, 0)
    return pl.pallas_call(
        _expert_kernel,
        grid_spec=pltpu.PrefetchScalarGridSpec(
            num_scalar_prefetch=2,
            grid=(rows // tr,),
            in_specs=[
                pl.BlockSpec((tr, SUBLANES, LANE), row_map),
                pl.BlockSpec((1, D_MODEL, D_FF), w_map),
                pl.BlockSpec((1, D_MODEL, D_FF), w_map),
                pl.BlockSpec((1, D_FF, D_MODEL), w_map),
            ],
            out_specs=pl.BlockSpec((tr, SUBLANES, LANE), out_map),
        ),
        out_shape=jax.ShapeDtypeStruct(xs.shape, F32),
        compiler_params=pltpu.CompilerParams(dimension_semantics=("arbitrary",)),
        name="experts",
    )(tile_expert, n_active, xs, wg, wu, wd)


def _combine_kernel(pos_ref, x1_ref, route_ref, ys_ref, g_ref, beta_ref, o_ref, buf_ref, sem):
    tm = x1_ref.shape[0]

    _row_copy_loop(tm, lambda r, p, k: pltpu.make_async_copy(ys_ref.at[pos_ref[p]], buf_ref.at[k, r], sem))
    for k in range(TOP_K):
        pltpu.make_async_copy(ys_ref.at[pl.ds(0, tm)], buf_ref.at[k], sem).wait()
    rt = route_ref[...]
    ffn = rt[:, TOP_K:TOP_K + 1] * buf_ref[0].reshape(tm, D_MODEL)
    for k in range(1, TOP_K):
        ffn = ffn + rt[:, TOP_K + k:TOP_K + k + 1] * buf_ref[k].reshape(tm, D_MODEL)
    o_ref[...] = _layer_norm(ALPHA * x1_ref[...] + ffn, g_ref[...], beta_ref[...])


def _combine_call(pos, x1, route, ys, g, beta):
    n = x1.shape[0]
    tm = TM_DISP
    vec = pl.BlockSpec((1, D_MODEL), lambda i: (0, 0))
    return pl.pallas_call(
        _combine_kernel,
        grid=(n // tm,),
        in_specs=[
            pl.BlockSpec((TOP_K * tm,), lambda i: (i,), memory_space=pltpu.SMEM),
            pl.BlockSpec((tm, D_MODEL), lambda i: (i, 0)),
            pl.BlockSpec((tm, LANE), lambda i: (i, 0)),
            pl.BlockSpec(memory_space=pl.ANY),
            vec, vec,
        ],
        out_specs=pl.BlockSpec((tm, D_MODEL), lambda i: (i, 0)),
        out_shape=jax.ShapeDtypeStruct((n, D_MODEL), F32),
        scratch_shapes=[pltpu.VMEM((TOP_K, tm, SUBLANES, LANE), F32), pltpu.SemaphoreType.DMA(())],
        compiler_params=pltpu.CompilerParams(dimension_semantics=("arbitrary",)),
        name="combine",
    )(pos, x1, route, ys, g, beta)


def _moe_call(x1, route, wg, wu, wd, g, beta):
    n = x1.shape[0]
    n_tiles = TOP_K * n // TR_MOE + N_EXPERTS
    pos, tile_expert, n_active = _route_positions(route, n_tiles)
    xs = _dispatch_call(pos, x1, jnp.zeros((n_tiles * TR_MOE, SUBLANES, LANE), F32))
    ys = _expert_call(tile_expert, n_active, xs, wg, wu, wd)
    return _combine_call(pos, x1, route, ys, g, beta)


def _rope_tables(T):
    half = MLA_ROPE // 2
    pos = jnp.arange(T, dtype=F32)
    inv = ROPE_BASE ** (-jnp.arange(half, dtype=F32) / half)
    ang = pos[:, None] * inv[None, :]
    cos, sin = jnp.cos(ang), jnp.sin(ang)
    one = jnp.ones((T, MLA_NOPE), F32)
    z16 = jnp.zeros((T, half), F32)
    z32 = jnp.zeros((T, LANE - MLA_NOPE - MLA_ROPE), F32)
    z64 = jnp.zeros((T, MLA_NOPE), F32)
    rc = jnp.concatenate([one, cos, cos, z32], axis=1)
    rs1 = jnp.concatenate([z64, z16, sin, z32], axis=1)
    rs2 = jnp.concatenate([z64, -sin, z16, z32], axis=1)
    return rc, rs1, rs2


def _alibi_features(T):
    c = jnp.asarray(SLOPES, F32) * LOG2E
    c1 = c.astype(BF16).astype(F32)
    c2 = (c - c1).astype(BF16).astype(F32)
    c3 = (c - c1 - c2).astype(BF16).astype(F32)
    width = LANE - DSA_DIM
    qfeat = jnp.stack([c1, c1, c2, c2, c3, c3], axis=1)
    qfeat = jnp.pad(qfeat, ((0, 0), (0, width - qfeat.shape[1])))
    pos = jnp.arange(T, dtype=I32)
    hi = ((pos // 256) * 256).astype(F32)
    lo = (pos % 256).astype(F32)
    kfeat = jnp.stack([hi, lo, hi, lo, hi, lo], axis=1)
    kfeat = jnp.pad(kfeat, ((0, 0), (0, width - kfeat.shape[1])))
    return qfeat, kfeat


def _pack_w_in(w):
    sizes = (Q_LORA, KV_LORA, MLA_ROPE, DSA_HEADS * DSA_DIM, DSA_DIM, DSA_DIM, IDX_HEADS * IDX_DIM, IDX_DIM, IDX_HEADS)
    offs = np.concatenate([[0], np.cumsum(sizes)])
    qa, kva, kr, dq, dk, dv, iq, ik, iw = [w[:, int(offs[j]):int(offs[j + 1])] for j in range(len(sizes))]
    z = lambda n: jnp.zeros((w.shape[0], n), w.dtype)
    cat = jnp.concatenate([
        qa, kva, dq, iq,
        z(MLA_NOPE), kr, z(LANE - MLA_NOPE - MLA_ROPE),
        dk, ik, z(LANE - DSA_DIM - IDX_DIM),
    ], axis=1)
    assert cat.shape[1] == C_END
    side = jnp.concatenate([dv, iw, z(R_END - DSA_DIM - IDX_HEADS)], axis=1).T
    return cat.astype(BF16), side.astype(BF16)


def kernel(x, w_in, q_norm_g, w_q_up, kv_norm_g, w_uk, w_uv, w_o, ln1_g, ln1_b, router_w, router_bias,
           w_gate, w_up, w_down, ln2_g, ln2_b):
    B, T, D = x.shape
    rc, rs1, rs2 = _rope_tables(T)
    rwt = router_w.T
    rwh = rwt.astype(BF16)
    rwl = (rwt - rwh.astype(F32)).astype(BF16)
    rb = router_bias.reshape(N_EXPERTS, 1).astype(F32)
    qfeat, kfeat = _alibi_features(T)
    for l in range(DEPTH):
        wcat, wside = _pack_w_in(w_in[l])
        wq = w_q_up[l].reshape(Q_LORA, MLA_HEADS, MLA_NOPE + MLA_ROPE).transpose(1, 0, 2)
        wq = jnp.pad(wq, ((0, 0), (0, 0), (0, HEAD_PAD - MLA_NOPE - MLA_ROPE))).astype(BF16)
        wk = w_uk[l].reshape(KV_LORA, MLA_HEADS, MLA_NOPE).transpose(1, 0, 2)
        wk = jnp.pad(wk, ((0, 0), (0, 0), (0, HEAD_PAD - MLA_NOPE))).astype(BF16)
        wvt = w_uv[l].reshape(KV_LORA, MLA_HEADS, MLA_V).transpose(1, 2, 0).astype(BF16)
        q, k, vt, dq, dk, dvt, iq, ik, iwt = _proj_call(
            x, wcat, wside, q_norm_g[l].reshape(1, Q_LORA), kv_norm_g[l].reshape(1, KV_LORA), wq, wk, wvt,
            rc, rs1, rs2, qfeat, kfeat)
        out_a = _mla_call(q, k, vt)
        out_b = _dsa_call(dq, iq, iwt, dk, dvt, ik)
        x1, route = _out_call(out_a, out_b, x, w_o[l].astype(BF16), ln1_g[l].reshape(1, D), ln1_b[l].reshape(1, D),
                              rwh, rwl, rb)
        y = _moe_call(x1.reshape(B * T, D), route.reshape(B * T, LANE), w_gate[l].astype(BF16), w_up[l].astype(BF16),
                      w_down[l].astype(BF16), ln2_g[l].reshape(1, D), ln2_b[l].reshape(1, D))
        x = y.reshape(B, T, D)
    return x
```

```python
import jax
import jax.numpy as jnp
import numpy as np
from jax import lax
from jax.experimental import pallas as pl
from jax.experimental.pallas import tpu as pltpu

F32 = jnp.float32
BF16 = jnp.bfloat16
I32 = jnp.int32

D_MODEL = 1024
DEPTH = 2
MLA_HEADS = 8
MLA_NOPE = 64
MLA_ROPE = 32
MLA_V = 64
Q_LORA = 384
KV_LORA = 256
ROPE_BASE = 10000.0
DSA_HEADS = 8
DSA_DIM = 64
IDX_HEADS = 8
IDX_DIM = 32
TOPK_MAX = 256
N_EXPERTS = 16
N_GROUPS = 4
EXPERTS_PER_GROUP = 4
D_FF = 512
ALPHA = (2.0 * DEPTH) ** 0.25
LN_EPS = 1e-5
RMS_EPS = 1e-6
MLA_SCALE = (MLA_NOPE + MLA_ROPE) ** -0.5
LOG2E = 1.4426950408889634
SLOPES = tuple(2.0 ** (-8.0 * (h + 1) / DSA_HEADS) for h in range(DSA_HEADS))

LANE = 128
SUBLANES = 8
HEAD_PAD = LANE
NEG = -1e30

C_QA = 0
C_KVA = C_QA + Q_LORA
C_DQ = C_KVA + KV_LORA
C_IQ = C_DQ + DSA_HEADS * DSA_DIM
C_KR = C_IQ + IDX_HEADS * IDX_DIM
C_KX = C_KR + LANE
C_END = C_KX + LANE
R_DV = 0
R_IW = DSA_DIM
R_END = LANE
IDX_SCALE = (IDX_DIM * IDX_HEADS) ** -0.5
DSA_SCALE = DSA_DIM ** -0.5

TM_PROJ = 512
TQ_MLA = 256
TQ_DSA = 256
TK_DSA = 256
TOP_K = 2
TR_MOE = 512
TM_DISP = 512

NT_DIMS = (((1,), (1,)), ((), ()))


def _rms(x, g):
    return x * lax.rsqrt(jnp.mean(x * x, axis=-1, keepdims=True) + RMS_EPS) * g


def _layer_norm(y, g, b):
    mu = jnp.mean(y, axis=-1, keepdims=True)
    d = y - mu
    var = jnp.mean(d * d, axis=-1, keepdims=True)
    return d * lax.rsqrt(var + LN_EPS) * g + b


def _proj_kernel(x_ref, wcat_ref, wside_ref, qg_ref, kvg_ref, wq_ref, wk_ref, wv_ref, rc_ref, rs1_ref, rs2_ref,
                 qfeat_ref, kfeat_ref,
                 q_ref, k_ref, vt_ref, dq_ref, dk_ref, dvt_ref, iq_ref, ik_ref, iwt_ref):
    xb = x_ref[0].astype(BF16)
    proj = jnp.dot(xb, wcat_ref[...], preferred_element_type=F32)
    side = lax.dot_general(wside_ref[...], xb, NT_DIMS, preferred_element_type=F32)
    qn = _rms(proj[:, C_QA:C_QA + Q_LORA], qg_ref[...]).astype(BF16)
    ckv = _rms(proj[:, C_KVA:C_KVA + KV_LORA], kvg_ref[...]).astype(BF16)
    rc, rs1, rs2 = rc_ref[...], rs1_ref[...], rs2_ref[...]

    def rope(t):
        return t * rc + pltpu.roll(t, 16, 1) * rs1 + pltpu.roll(t, LANE - 16, 1) * rs2

    kr = rope(proj[:, C_KR:C_KR + LANE])
    tm = xb.shape[0]
    for h in range(MLA_HEADS):
        qh = jnp.dot(qn, wq_ref[h], preferred_element_type=F32)
        q_ref[0, h] = (rope(qh) * (MLA_SCALE * LOG2E)).astype(BF16)
        kh = jnp.dot(ckv, wk_ref[h], preferred_element_type=F32) + kr
        k_ref[0, h] = kh.astype(BF16)
        vth = lax.dot_general(wv_ref[h], ckv, NT_DIMS, preferred_element_type=F32)
        for j in range(tm // TQ_MLA):
            vt_ref[0, h, j] = vth[:, TQ_MLA * j:TQ_MLA * (j + 1)].astype(BF16)
    for h in range(DSA_HEADS):
        dqh = proj[:, C_DQ + DSA_DIM * h:C_DQ + DSA_DIM * (h + 1)] * (DSA_SCALE * LOG2E)
        feat = jnp.broadcast_to(qfeat_ref[h:h + 1, :], (tm, LANE - DSA_DIM))
        dq_ref[0, h] = jnp.concatenate([dqh, feat], axis=1).astype(BF16)
    for h in range(IDX_HEADS):
        iq_ref[0, h] = proj[:, C_IQ + IDX_DIM * h:C_IQ + IDX_DIM * (h + 1)].astype(BF16)
    dk_ref[0] = jnp.concatenate([proj[:, C_KX:C_KX + DSA_DIM], kfeat_ref[...]], axis=1).astype(BF16)
    ik_ref[0] = proj[:, C_KX + DSA_DIM:C_KX + DSA_DIM + IDX_DIM].astype(BF16)
    for j in range(TM_PROJ // TK_DSA):
        dvt_ref[0, j] = side[R_DV:R_DV + DSA_DIM, TK_DSA * j:TK_DSA * (j + 1)].astype(BF16)
    iwt_ref[0] = side[R_IW:R_IW + IDX_HEADS, :] * IDX_SCALE


def _proj_call(x, wcat, wside, qg, kvg, wq, wk, wv, rc, rs1, rs2, qfeat, kfeat):
    B, T, _ = x.shape
    tm = TM_PROJ
    full = lambda shape: pl.BlockSpec(shape, lambda b, t: (0,) * len(shape))
    head_out = lambda d: pl.BlockSpec((1, MLA_HEADS, tm, d), lambda b, t: (b, 0, t, 0))
    tok_out = lambda d: pl.BlockSpec((1, tm, d), lambda b, t: (b, t, 0))
    tab = pl.BlockSpec((tm, LANE), lambda b, t: (t, 0))
    return pl.pallas_call(
        _proj_kernel,
        grid=(B, T // tm),
        in_specs=[
            pl.BlockSpec((1, tm, D_MODEL), lambda b, t: (b, t, 0)),
            full(wcat.shape), full(wside.shape), full(qg.shape), full(kvg.shape), full(wq.shape), full(wk.shape),
            full(wv.shape), tab, tab, tab, full(qfeat.shape),
            pl.BlockSpec((tm, LANE - DSA_DIM), lambda b, t: (t, 0)),
        ],
        out_specs=[head_out(HEAD_PAD), head_out(HEAD_PAD),
                   pl.BlockSpec((1, MLA_HEADS, tm // TQ_MLA, MLA_V, TQ_MLA), lambda b, t: (b, 0, t, 0, 0)),
                   head_out(LANE),
                   tok_out(LANE),
                   pl.BlockSpec((1, tm // TK_DSA, DSA_DIM, TK_DSA), lambda b, t: (b, t, 0, 0)),
                   head_out(IDX_DIM), tok_out(IDX_DIM),
                   pl.BlockSpec((1, IDX_HEADS, tm), lambda b, t: (b, 0, t))],
        out_shape=[
            jax.ShapeDtypeStruct((B, MLA_HEADS, T, HEAD_PAD), BF16),
            jax.ShapeDtypeStruct((B, MLA_HEADS, T, HEAD_PAD), BF16),
            jax.ShapeDtypeStruct((B, MLA_HEADS, T // TQ_MLA, MLA_V, TQ_MLA), BF16),
            jax.ShapeDtypeStruct((B, DSA_HEADS, T, LANE), BF16),
            jax.ShapeDtypeStruct((B, T, LANE), BF16),
            jax.ShapeDtypeStruct((B, T // TK_DSA, DSA_DIM, TK_DSA), BF16),
            jax.ShapeDtypeStruct((B, IDX_HEADS, T, IDX_DIM), BF16),
            jax.ShapeDtypeStruct((B, T, IDX_DIM), BF16),
            jax.ShapeDtypeStruct((B, IDX_HEADS, T), F32),
        ],
        compiler_params=pltpu.CompilerParams(dimension_semantics=("parallel", "parallel")),
        name="proj",
    )(x, wcat, wside, qg, kvg, wq, wk, wv, rc, rs1, rs2, qfeat, kfeat)


def _transposed_heads_store(o_ref, ot, heads, width, tq):
    zpad = jnp.zeros((LANE - width, tq), F32)
    for h in range(heads):
        oh = jnp.concatenate([ot[:, tq * h:tq * (h + 1)], zpad], axis=0).T
        o_ref[0, :, width * h:width * (h + 1)] = oh[:, 0:width].astype(BF16)


def _mla_kernel(q_ref, k_ref, vt_ref, o_ref, m_ref, l_ref, acc_ref, base_ref):
    i = pl.program_id(1)
    tq = TQ_MLA
    heads = MLA_HEADS
    m_ref[...] = jnp.full(m_ref.shape, NEG, F32)
    l_ref[...] = jnp.zeros(l_ref.shape, F32)
    acc_ref[...] = jnp.zeros(acc_ref.shape, F32)
    key = lax.broadcasted_iota(I32, (tq, tq), 0)
    qpos = lax.broadcasted_iota(I32, (tq, tq), 1)
    base_ref[...] = jnp.where(key <= qpos, -NEG, NEG)

    def step(j, masked):
        start = pl.multiple_of(j * tq, tq)
        s_list = [lax.dot_general(k_ref[0, h, pl.ds(start, tq), :], q_ref[0, h], NT_DIMS,
                                  preferred_element_type=F32) for h in range(heads)]
        m_all, l_all, acc_all = m_ref[...], l_ref[...], acc_ref[...]
        m_out, l_out, a_out, pv_out = [], [], [], []
        for h in range(heads):
            lanes = slice(tq * h, tq * (h + 1))
            s = jnp.minimum(s_list[h], base_ref[...]) if masked else s_list[h]
            m_prev = m_all[:, lanes]
            m_new = jnp.maximum(m_prev, jnp.max(s, axis=0, keepdims=True))
            p = jnp.exp2(s - m_new)
            a = jnp.exp2(m_prev - m_new)
            l_out.append(a * l_all[:, lanes] + jnp.sum(p, axis=0, keepdims=True))
            pv_out.append(jnp.dot(vt_ref[0, h, j], p.astype(BF16), preferred_element_type=F32))
            a_out.append(a)
            m_out.append(m_new)
        m_ref[...] = jnp.concatenate(m_out, axis=1)
        l_ref[...] = jnp.concatenate(l_out, axis=1)
        acc_ref[...] = jnp.concatenate(a_out, axis=1) * acc_all + jnp.concatenate(pv_out, axis=1)

    def body(j, carry):
        step(j, False)
        return carry

    def body_diag(j, carry):
        step(j, True)
        return carry

    lax.fori_loop(0, i, body, 0)
    lax.fori_loop(i, i + 1, body_diag, 0)
    _transposed_heads_store(o_ref, acc_ref[...] / l_ref[...], heads, MLA_V, tq)


def _mla_call(q, k, vt):
    B, H, T, _ = q.shape
    tq = TQ_MLA
    return pl.pallas_call(
        _mla_kernel,
        grid=(B, T // tq),
        in_specs=[
            pl.BlockSpec((1, H, tq, HEAD_PAD), lambda b, i: (b, 0, i, 0)),
            pl.BlockSpec((1, H, T, HEAD_PAD), lambda b, i: (b, 0, 0, 0)),
            pl.BlockSpec((1, H, T // tq, MLA_V, tq), lambda b, i: (b, 0, 0, 0, 0)),
        ],
        out_specs=pl.BlockSpec((1, tq, H * MLA_V), lambda b, i: (b, i, 0)),
        out_shape=jax.ShapeDtypeStruct((B, T, H * MLA_V), BF16),
        scratch_shapes=[
            pltpu.VMEM((1, H * tq), F32),
            pltpu.VMEM((1, H * tq), F32),
            pltpu.VMEM((MLA_V, H * tq), F32),
            pltpu.VMEM((tq, tq), F32),
        ],
        compiler_params=pltpu.CompilerParams(dimension_semantics=("parallel", "arbitrary")),
        name="mla",
    )(q, k, vt)


def _dsa_kernel(dq_ref, iq_ref, iwt_ref, dk_ref, dvt_ref, ik_ref, o_ref, sc_ref, m_ref, l_ref, acc_ref):
    i = pl.program_id(1)
    tq, tk = TQ_DSA, TK_DSA
    T = dk_ref.shape[1]
    nch = lax.div((i + 1) * tq + (tk - 1), tk)
    cols = DSA_HEADS * tq
    key = lax.broadcasted_iota(I32, (tk, tq), 0)
    t_q = i * tq + lax.broadcasted_iota(I32, (tk, tq), 1)
    t_row = i * tq + lax.broadcasted_iota(I32, (1, tq), 1)

    iq2 = iq_ref[0].reshape(cols, IDX_DIM)
    w8 = iwt_ref[0]

    def score_body(c, carry):
        start = pl.multiple_of(c * tk, tk)
        ikc = ik_ref[0, pl.ds(start, tk), :]
        rel = lax.dot_general(ikc, iq2, NT_DIMS, preferred_element_type=F32)
        sc = jnp.maximum(rel[:, 0:tq], 0.0) * w8[0:1]
        for h in range(1, IDX_HEADS):
            sc = sc + jnp.maximum(rel[:, tq * h:tq * (h + 1)], 0.0) * w8[h:h + 1]
        sc_ref[c] = jnp.where(key + c * tk <= t_q, sc, -jnp.inf)
        return carry

    lax.fori_loop(0, nch, score_body, 0)

    def count(pred):
        def hits(c):
            hit = jnp.where(pred(sc_ref[c], key + c * tk), 1.0, 0.0)
            return jnp.sum(hit.reshape(tk // SUBLANES, SUBLANES, tq), axis=0)

        def pair_body(j, acc):
            return acc + hits(2 * j) + hits(2 * j + 1)

        acc = lax.fori_loop(0, lax.shift_right_logical(nch, 1), pair_body, jnp.zeros((SUBLANES, tq), F32))
        acc = acc + lax.cond((nch & 1) == 1, lambda: hits(nch - 1), lambda: jnp.zeros((SUBLANES, tq), F32))
        return jnp.sum(acc, axis=0, keepdims=True)

    int_min = jnp.int32(-2 ** 31)

    def key_to_f32(ku):
        ks = ku ^ int_min
        bits = jnp.where(ks >= 0, ks, ks ^ jnp.int32(0x7FFFFFFF))
        return lax.bitcast_convert_type(bits, F32)

    kf = float(TOPK_MAX)

    def bis_body(step, carry):
        res, cnt_res = carry
        cand = res | lax.shift_left(jnp.int32(1), 31 - step)
        tau_c = key_to_f32(cand)
        cnt = count(lambda sc, idx: sc >= tau_c)
        ok = cnt >= kf
        return jnp.where(ok, cand, res), jnp.where(ok, cnt, cnt_res)

    res, cnt_ge = lax.fori_loop(0, 32, bis_body, (jnp.zeros((1, tq), I32), jnp.full((1, tq), kf, F32)))
    short = t_row < TOPK_MAX - 1
    tau = jnp.where(short, -jnp.inf, key_to_f32(res))
    cnt_ge = jnp.where(short, kf, cnt_ge)

    def tie_search():
        need = float(TOPK_MAX) - count(lambda sc, idx: sc > tau)

        def tie_body(step, jm):
            cand = jm | lax.shift_left(jnp.int32(1), 10 - step)
            cnt = count(lambda sc, idx: (sc >= tau) & jnp.logical_not(sc > tau) & (idx < cand))
            return jnp.where(cnt < need, cand, jm)

        return lax.fori_loop(0, 11, tie_body, jnp.zeros((1, tq), I32))

    j_max = lax.cond(jnp.max(cnt_ge) > float(TOPK_MAX), tie_search, lambda: jnp.full((1, tq), T, I32))

    q2 = dq_ref[0].reshape(cols, LANE)
    m_ref[...] = jnp.full((1, cols), NEG, F32)
    l_ref[...] = jnp.zeros((1, cols), F32)
    acc_ref[...] = jnp.zeros((DSA_DIM, cols), F32)

    def att_body(c, carry):
        start = pl.multiple_of(c * tk, tk)
        kc = dk_ref[0, pl.ds(start, tk), :]
        vt = dvt_ref[0, c]
        sc = sc_ref[c]
        idx = key + c * tk
        sel = (sc >= tau) & ((sc > tau) | (idx <= j_max)) & (sc > -jnp.inf)
        cap = jnp.where(sel, -NEG, NEG)
        m_all, l_all, acc_all = m_ref[...], l_ref[...], acc_ref[...]
        m_out, l_out, a_out, p_out = [], [], [], []
        s_all = lax.dot_general(kc, q2, NT_DIMS, preferred_element_type=F32)
        for h in range(DSA_HEADS):
            lanes = slice(tq * h, tq * (h + 1))
            s = jnp.minimum(s_all[:, lanes], cap)
            m_prev = m_all[:, lanes]
            m_new = jnp.maximum(m_prev, jnp.max(s, axis=0, keepdims=True))
            p = jnp.exp2(s - m_new)
            a = jnp.exp2(m_prev - m_new)
            l_out.append(a * l_all[:, lanes] + jnp.sum(p, axis=0, keepdims=True))
            p_out.append(p.astype(BF16))
            a_out.append(a)
            m_out.append(m_new)
        pv = jnp.dot(vt, jnp.concatenate(p_out, axis=1), preferred_element_type=F32)
        m_ref[...] = jnp.concatenate(m_out, axis=1)
        l_ref[...] = jnp.concatenate(l_out, axis=1)
        acc_ref[...] = jnp.concatenate(a_out, axis=1) * acc_all + pv
        return carry

    lax.fori_loop(0, nch, att_body, 0)
    _transposed_heads_store(o_ref, acc_ref[...] / l_ref[...], DSA_HEADS, DSA_DIM, tq)


def _dsa_call(dq, iq, iwt, dk, dvt, ik):
    B, H, T, _ = dq.shape
    tq, tk = TQ_DSA, TK_DSA
    cols = H * tq
    return pl.pallas_call(
        _dsa_kernel,
        grid=(B, T // tq),
        in_specs=[
            pl.BlockSpec((1, H, tq, LANE), lambda b, i: (b, 0, i, 0)),
            pl.BlockSpec((1, H, tq, IDX_DIM), lambda b, i: (b, 0, i, 0)),
            pl.BlockSpec((1, IDX_HEADS, tq), lambda b, i: (b, 0, i)),
            pl.BlockSpec((1, T, LANE), lambda b, i: (b, 0, 0)),
            pl.BlockSpec((1, T // tk, DSA_DIM, tk), lambda b, i: (b, 0, 0, 0)),
            pl.BlockSpec((1, T, IDX_DIM), lambda b, i: (b, 0, 0)),
        ],
        out_specs=pl.BlockSpec((1, tq, H * DSA_DIM), lambda b, i: (b, i, 0)),
        out_shape=jax.ShapeDtypeStruct((B, T, H * DSA_DIM), BF16),
        scratch_shapes=[
            pltpu.VMEM((T // tk, tk, tq), F32),
            pltpu.VMEM((1, cols), F32),
            pltpu.VMEM((1, cols), F32),
            pltpu.VMEM((DSA_DIM, cols), F32),
        ],
        compiler_params=pltpu.CompilerParams(dimension_semantics=("parallel", "arbitrary")),
        name="dsa",
    )(dq, iq, iwt, dk, dvt, ik)


def _route(aff, biased):
    b = [biased[e:e + 1] for e in range(N_EXPERTS)]
    a = [aff[e:e + 1] for e in range(N_EXPERTS)]
    gs = []
    for g in range(N_GROUPS):
        v0, v1, v2, v3 = b[4 * g:4 * g + 4]
        hi1, lo1 = jnp.maximum(v0, v1), jnp.minimum(v0, v1)
        hi2, lo2 = jnp.maximum(v2, v3), jnp.minimum(v2, v3)
        gs.append(jnp.maximum(hi1, hi2) + jnp.maximum(jnp.minimum(hi1, hi2), jnp.maximum(lo1, lo2)))
    best, sel = gs[0], jnp.zeros_like(gs[0], dtype=I32)
    for g in range(1, N_GROUPS):
        upd = gs[g] > best
        sel = jnp.where(upd, g, sel)
        best = jnp.where(upd, gs[g], best)
    v, av = [], []
    for k in range(EXPERTS_PER_GROUP):
        vk, ak = b[k], a[k]
        for g in range(1, N_GROUPS):
            vk = jnp.where(sel == g, b[4 * g + k], vk)
            ak = jnp.where(sel == g, a[4 * g + k], ak)
        v.append(vk)
        av.append(ak)
    m1, i1 = v[0], jnp.zeros_like(sel)
    for k in range(1, EXPERTS_PER_GROUP):
        upd = v[k] > m1
        i1 = jnp.where(upd, k, i1)
        m1 = jnp.where(upd, v[k], m1)
    m2, i2 = jnp.full_like(m1, -jnp.inf), jnp.zeros_like(sel)
    for k in range(EXPERTS_PER_GROUP):
        cand = jnp.where(i1 == k, -jnp.inf, v[k])
        upd = cand > m2
        i2 = jnp.where(upd, k, i2)
        m2 = jnp.where(upd, cand, m2)
    a1, a2 = jnp.zeros_like(m1), jnp.zeros_like(m1)
    for k in range(EXPERTS_PER_GROUP):
        a1 = jnp.where(i1 == k, av[k], a1)
        a2 = jnp.where(i2 == k, av[k], a2)
    tot = a1 + a2
    e1 = sel * EXPERTS_PER_GROUP + i1
    e2 = sel * EXPERTS_PER_GROUP + i2
    return e1, e2, a1 / tot, a2 / tot


def _out_kernel(a_ref, b_ref, x_ref, wo_ref, g_ref, beta_ref, rwh_ref, rwl_ref, rb_ref, x1_ref, route_ref):
    half = MLA_HEADS * MLA_V
    mix = jnp.dot(a_ref[0], wo_ref[0:half, :], preferred_element_type=F32)
    mix = mix + jnp.dot(b_ref[0], wo_ref[half:2 * half, :], preferred_element_type=F32)
    x1 = _layer_norm(ALPHA * x_ref[0] + mix, g_ref[...], beta_ref[...])
    x1_ref[0] = x1
    xh = x1.astype(BF16)
    xl = (x1 - xh.astype(F32)).astype(BF16)
    rwh, rwl = rwh_ref[...], rwl_ref[...]
    logits = lax.dot_general(rwh, xh, NT_DIMS, preferred_element_type=F32)
    logits = logits + lax.dot_general(rwh, xl, NT_DIMS, preferred_element_type=F32)
    logits = logits + lax.dot_general(rwl, xh, NT_DIMS, preferred_element_type=F32)
    aff = 1.0 / (1.0 + jnp.exp(-logits))
    e1, e2, w1, w2 = _route(aff, aff + rb_ref[...])
    tm = logits.shape[1]
    rows = [e1.astype(F32), e2.astype(F32), w1, w2, jnp.zeros((LANE - 4, tm), F32)]
    route_ref[0] = jnp.concatenate(rows, axis=0).T


def _out_call(a, b, x, wo, g, beta, rwh, rwl, rb):
    B, T, _ = x.shape
    tm = TM_PROJ
    full = lambda shape: pl.BlockSpec(shape, lambda bb, t: (0,) * len(shape))
    tok = lambda d: pl.BlockSpec((1, tm, d), lambda bb, t: (bb, t, 0))
    return pl.pallas_call(
        _out_kernel,
        grid=(B, T // tm),
        in_specs=[tok(a.shape[-1]), tok(b.shape[-1]), tok(D_MODEL), full(wo.shape), full(g.shape), full(beta.shape),
                  full(rwh.shape), full(rwl.shape), full(rb.shape)],
        out_specs=[tok(D_MODEL), tok(LANE)],
        out_shape=[jax.ShapeDtypeStruct((B, T, D_MODEL), F32), jax.ShapeDtypeStruct((B, T, LANE), F32)],
        compiler_params=pltpu.CompilerParams(dimension_semantics=("parallel", "parallel")),
        name="outproj",
    )(a, b, x, wo, g, beta, rwh, rwl, rb)


def _route_positions(route, n_tiles):
    eid = route[:, 0:TOP_K].astype(I32).reshape(-1)
    onehot = (eid[:, None] == jnp.arange(N_EXPERTS, dtype=I32)[None, :]).astype(I32)
    csum = jnp.cumsum(onehot, axis=0)
    rank = jnp.sum(onehot * csum, axis=1) - 1
    counts = csum[-1]
    padded = ((counts + TR_MOE - 1) // TR_MOE) * TR_MOE
    ends = jnp.cumsum(padded)
    starts = ends - padded
    pos = jnp.sum(onehot * starts[None, :], axis=1) + rank
    n_active = ends[-1] // TR_MOE
    tile_row = jnp.minimum(jnp.arange(n_tiles, dtype=I32), n_active - 1) * TR_MOE
    tile_expert = jnp.sum((tile_row[:, None] >= ends[None, :]).astype(I32), axis=1)
    return pos.astype(I32), tile_expert.astype(I32), n_active.reshape(1).astype(I32)


def _row_copy_loop(tm, copy):
    def body(q, carry):
        base = pl.multiple_of(q * SUBLANES, SUBLANES)
        for j in range(SUBLANES):
            for k in range(TOP_K):
                copy(base + j, (base + j) * TOP_K + k, k).start(priority=k)
        return carry

    lax.fori_loop(0, tm // SUBLANES, body, 0)


def _dispatch_kernel(pos_ref, x_ref, xs_in_ref, xs_ref, xr_ref, sem):
    del xs_in_ref
    tm = x_ref.shape[0]
    xr_ref[...] = x_ref[...].reshape(tm, SUBLANES, LANE)
    _row_copy_loop(tm, lambda r, p, k: pltpu.make_async_copy(xr_ref.at[r], xs_ref.at[pos_ref[p]], sem))
    for k in range(TOP_K):
        pltpu.make_async_copy(xr_ref, xs_ref.at[pl.ds(0, tm)], sem).wait()


def _dispatch_call(pos, x1, xs_zero):
    n = x1.shape[0]
    tm = TM_DISP
    return pl.pallas_call(
        _dispatch_kernel,
        grid=(n // tm,),
        in_specs=[
            pl.BlockSpec((TOP_K * tm,), lambda i: (i,), memory_space=pltpu.SMEM),
            pl.BlockSpec((tm, D_MODEL), lambda i: (i, 0)),
            pl.BlockSpec(memory_space=pl.ANY),
        ],
        out_specs=pl.BlockSpec(memory_space=pl.ANY),
        out_shape=jax.ShapeDtypeStruct(xs_zero.shape, xs_zero.dtype),
        scratch_shapes=[pltpu.VMEM((tm, SUBLANES, LANE), F32), pltpu.SemaphoreType.DMA(())],
        input_output_aliases={2: 0},
        compiler_params=pltpu.CompilerParams(dimension_semantics=("arbitrary",)),
        name="dispatch",
    )(pos, x1, xs_zero)


def _expert_kernel(te_ref, na_ref, xs_ref, wg_ref, wu_ref, wd_ref, ys_ref):
    del te_ref
    active = pl.program_id(0) < na_ref[0]

    @pl.when(active)
    def _():
        tr = xs_ref.shape[0]
        xb = xs_ref[...].reshape(tr, D_MODEL).astype(BF16)
        hg = jnp.dot(xb, wg_ref[0], preferred_element_type=F32)
        hu = jnp.dot(xb, wu_ref[0], preferred_element_type=F32)
        act = (hg / (1.0 + jnp.exp(-hg))) * hu
        y = jnp.dot(act.astype(BF16), wd_ref[0], preferred_element_type=F32)
        ys_ref[...] = y.reshape(tr, SUBLANES, LANE)

    @pl.when(jnp.logical_not(active))
    def _():
        ys_ref[...] = jnp.zeros_like(ys_ref)


def _expert_call(tile_expert, n_active, xs, wg, wu, wd):
    rows = xs.shape[0]
    tr = TR_MOE
    row_map = lambda i, te, na: (jnp.minimum(i, na[0] - 1), 0, 0)
    out_map = lambda i, te, na: (i, 0, 0)
    w_map = lambda i, te, na: (te[i], 0, 0)
    return pl.pallas_call(
        _expert_kernel,
        grid_spec=pltpu.PrefetchScalarGridSpec(
            num_scalar_prefetch=2,
            grid=(rows // tr,),
            in_specs=[
                pl.BlockSpec((tr, SUBLANES, LANE), row_map),
                pl.BlockSpec((1, D_MODEL, D_FF), w_map),
                pl.BlockSpec((1, D_MODEL, D_FF), w_map),
                pl.BlockSpec((1, D_FF, D_MODEL), w_map),
            ],
            out_specs=pl.BlockSpec((tr, SUBLANES, LANE), out_map),
        ),
        out_shape=jax.ShapeDtypeStruct(xs.shape, F32),
        compiler_params=pltpu.CompilerParams(dimension_semantics=("arbitrary",)),
        name="experts",
    )(tile_expert, n_active, xs, wg, wu, wd)


def _combine_kernel(pos_ref, x1_ref, route_ref, ys_ref, g_ref, beta_ref, o_ref, buf_ref, sem):
    tm = x1_ref.shape[0]

    _row_copy_loop(tm, lambda r, p, k: pltpu.make_async_copy(ys_ref.at[pos_ref[p]], buf_ref.at[k, r], sem))
    for k in range(TOP_K):
        pltpu.make_async_copy(ys_ref.at[pl.ds(0, tm)], buf_ref.at[k], sem).wait()
    rt = route_ref[...]
    ffn = rt[:, TOP_K:TOP_K + 1] * buf_ref[0].reshape(tm, D_MODEL)
    for k in range(1, TOP_K):
        ffn = ffn + rt[:, TOP_K + k:TOP_K + k + 1] * buf_ref[k].reshape(tm, D_MODEL)
    o_ref[...] = _layer_norm(ALPHA * x1_ref[...] + ffn, g_ref[...], beta_ref[...])


def _combine_call(pos, x1, route, ys, g, beta):
    n = x1.shape[0]
    tm = TM_DISP
    vec = pl.BlockSpec((1, D_MODEL), lambda i: (0, 0))
    return pl.pallas_call(
        _combine_kernel,
        grid=(n // tm,),
        in_specs=[
            pl.BlockSpec((TOP_K * tm,), lambda i: (i,), memory_space=pltpu.SMEM),
            pl.BlockSpec((tm, D_MODEL), lambda i: (i, 0)),
            pl.BlockSpec((tm, LANE), lambda i: (i, 0)),
            pl.BlockSpec(memory_space=pl.ANY),
            vec, vec,
        ],
        out_specs=pl.BlockSpec((tm, D_MODEL), lambda i: (i, 0)),
        out_shape=jax.ShapeDtypeStruct((n, D_MODEL), F32),
        scratch_shapes=[pltpu.VMEM((TOP_K, tm, SUBLANES, LANE), F32), pltpu.SemaphoreType.DMA(())],
        compiler_params=pltpu.CompilerParams(dimension_semantics=("arbitrary",)),
        name="combine",
    )(pos, x1, route, ys, g, beta)


def _moe_call(x1, route, wg, wu, wd, g, beta):
    n = x1.shape[0]
    n_tiles = TOP_K * n // TR_MOE + N_EXPERTS
    pos, tile_expert, n_active = _route_positions(route, n_tiles)
    xs = _dispatch_call(pos, x1, jnp.zeros((n_tiles * TR_MOE, SUBLANES, LANE), F32))
    ys = _expert_call(tile_expert, n_active, xs, wg, wu, wd)
    return _combine_call(pos, x1, route, ys, g, beta)


def _rope_tables(T):
    half = MLA_ROPE // 2
    pos = jnp.arange(T, dtype=F32)
    inv = ROPE_BASE ** (-jnp.arange(half, dtype=F32) / half)
    ang = pos[:, None] * inv[None, :]
    cos, sin = jnp.cos(ang), jnp.sin(ang)
    one = jnp.ones((T, MLA_NOPE), F32)
    z16 = jnp.zeros((T, half), F32)
    z32 = jnp.zeros((T, LANE - MLA_NOPE - MLA_ROPE), F32)
    z64 = jnp.zeros((T, MLA_NOPE), F32)
    rc = jnp.concatenate([one, cos, cos, z32], axis=1)
    rs1 = jnp.concatenate([z64, z16, sin, z32], axis=1)
    rs2 = jnp.concatenate([z64, -sin, z16, z32], axis=1)
    return rc, rs1, rs2


def _alibi_features(T):
    c = jnp.asarray(SLOPES, F32) * LOG2E
    c1 = c.astype(BF16).astype(F32)
    c2 = (c - c1).astype(BF16).astype(F32)
    c3 = (c - c1 - c2).astype(BF16).astype(F32)
    width = LANE - DSA_DIM
    qfeat = jnp.stack([c1, c1, c2, c2, c3, c3], axis=1)
    qfeat = jnp.pad(qfeat, ((0, 0), (0, width - qfeat.shape[1])))
    pos = jnp.arange(T, dtype=I32)
    hi = ((pos // 256) * 256).astype(F32)
    lo = (pos % 256).astype(F32)
    kfeat = jnp.stack([hi, lo, hi, lo, hi, lo], axis=1)
    kfeat = jnp.pad(kfeat, ((0, 0), (0, width - kfeat.shape[1])))
    return qfeat, kfeat


def _pack_w_in(w):
    sizes = (Q_LORA, KV_LORA, MLA_ROPE, DSA_HEADS * DSA_DIM, DSA_DIM, DSA_DIM, IDX_HEADS * IDX_DIM, IDX_DIM, IDX_HEADS)
    offs = np.concatenate([[0], np.cumsum(sizes)])
    w = w.astype(BF16)
    qa, kva, kr, dq, dk, dv, iq, ik, iw = [w[:, int(offs[j]):int(offs[j + 1])] for j in range(len(sizes))]
    z = lambda n: jnp.zeros((w.shape[0], n), w.dtype)
    cat = jnp.concatenate([
        qa, kva, dq, iq,
        z(MLA_NOPE), kr, z(LANE - MLA_NOPE - MLA_ROPE),
        dk, ik, z(LANE - DSA_DIM - IDX_DIM),
    ], axis=1)
    assert cat.shape[1] == C_END
    side = jnp.concatenate([dv, iw, z(R_END - DSA_DIM - IDX_HEADS)], axis=1).T
    return cat, side


def kernel(x, w_in, q_norm_g, w_q_up, kv_norm_g, w_uk, w_uv, w_o, ln1_g, ln1_b, router_w, router_bias,
           w_gate, w_up, w_down, ln2_g, ln2_b):
    B, T, D = x.shape
    rc, rs1, rs2 = _rope_tables(T)
    rwt = router_w.T
    rwh = rwt.astype(BF16)
    rwl = (rwt - rwh.astype(F32)).astype(BF16)
    rb = router_bias.reshape(N_EXPERTS, 1).astype(F32)
    qfeat, kfeat = _alibi_features(T)
    for l in range(DEPTH):
        wcat, wside = _pack_w_in(w_in[l])
        wq = w_q_up[l].reshape(Q_LORA, MLA_HEADS, MLA_NOPE + MLA_ROPE).transpose(1, 0, 2)
        wq = jnp.pad(wq, ((0, 0), (0, 0), (0, HEAD_PAD - MLA_NOPE - MLA_ROPE))).astype(BF16)
        wk = w_uk[l].reshape(KV_LORA, MLA_HEADS, MLA_NOPE).transpose(1, 0, 2)
        wk = jnp.pad(wk, ((0, 0), (0, 0), (0, HEAD_PAD - MLA_NOPE))).astype(BF16)
        wvt = w_uv[l].reshape(KV_LORA, MLA_HEADS, MLA_V).transpose(1, 2, 0).astype(BF16)
        q, k, vt, dq, dk, dvt, iq, ik, iwt = _proj_call(
            x, wcat, wside, q_norm_g[l].reshape(1, Q_LORA), kv_norm_g[l].reshape(1, KV_LORA), wq, wk, wvt,
            rc, rs1, rs2, qfeat, kfeat)
        out_a = _mla_call(q, k, vt)
        out_b = _dsa_call(dq, iq, iwt, dk, dvt, ik)
        x1, route = _out_call(out_a, out_b, x, w_o[l].astype(BF16), ln1_g[l].reshape(1, D), ln1_b[l].reshape(1, D),
                              rwh, rwl, rb)
        y = _moe_call(x1.reshape(B * T, D), route.reshape(B * T, LANE), w_gate[l].astype(BF16), w_up[l].astype(BF16),
                      w_down[l].astype(BF16), ln2_g[l].reshape(1, D), ln2_b[l].reshape(1, D))
        x = y.reshape(B, T, D)
    return x
```

```python
import jax
import jax.numpy as jnp
import numpy as np
from jax import lax
from jax.experimental import pallas as pl
from jax.experimental.pallas import tpu as pltpu

F32 = jnp.float32
BF16 = jnp.bfloat16
I32 = jnp.int32

D_MODEL = 1024
DEPTH = 2
MLA_HEADS = 8
MLA_NOPE = 64
MLA_ROPE = 32
MLA_V = 64
Q_LORA = 384
KV_LORA = 256
ROPE_BASE = 10000.0
DSA_HEADS = 8
DSA_DIM = 64
IDX_HEADS = 8
IDX_DIM = 32
TOPK_MAX = 256
N_EXPERTS = 16
N_GROUPS = 4
EXPERTS_PER_GROUP = 4
D_FF = 512
ALPHA = (2.0 * DEPTH) ** 0.25
LN_EPS = 1e-5
RMS_EPS = 1e-6
MLA_SCALE = (MLA_NOPE + MLA_ROPE) ** -0.5
LOG2E = 1.4426950408889634
SLOPES = tuple(2.0 ** (-8.0 * (h + 1) / DSA_HEADS) for h in range(DSA_HEADS))

LANE = 128
SUBLANES = 8
HEAD_PAD = LANE
NEG = -1e30

C_QA = 0
C_KVA = C_QA + Q_LORA
C_DQ = C_KVA + KV_LORA
C_IQ = C_DQ + DSA_HEADS * DSA_DIM
C_KR = C_IQ + IDX_HEADS * IDX_DIM
C_KX = C_KR + LANE
C_END = C_KX + LANE
R_DV = 0
R_IW = DSA_DIM
R_END = LANE
IDX_SCALE = (IDX_DIM * IDX_HEADS) ** -0.5
DSA_SCALE = DSA_DIM ** -0.5

TM_PROJ = 512
TQ_MLA = 256
TQ_DSA = 256
TK_DSA = 256
TOP_K = 2
TR_MOE = 512
TM_DISP = 512

NT_DIMS = (((1,), (1,)), ((), ()))


def _rms(x, g):
    return x * lax.rsqrt(jnp.mean(x * x, axis=-1, keepdims=True) + RMS_EPS) * g


def _layer_norm(y, g, b):
    mu = jnp.mean(y, axis=-1, keepdims=True)
    d = y - mu
    var = jnp.mean(d * d, axis=-1, keepdims=True)
    return d * lax.rsqrt(var + LN_EPS) * g + b


def _proj_kernel(x_ref, wcat_ref, wside_ref, qg_ref, kvg_ref, wq_ref, wk_ref, wv_ref, rc_ref, rs1_ref, rs2_ref,
                 qfeat_ref, kfeat_ref,
                 q_ref, k_ref, vt_ref, dq_ref, dk_ref, dvt_ref, iq_ref, ik_ref, iwt_ref):
    xb = x_ref[0].astype(BF16)
    proj = jnp.dot(xb, wcat_ref[...], preferred_element_type=F32)
    side = lax.dot_general(wside_ref[...], xb, NT_DIMS, preferred_element_type=F32)
    qn = _rms(proj[:, C_QA:C_QA + Q_LORA], qg_ref[...]).astype(BF16)
    ckv = _rms(proj[:, C_KVA:C_KVA + KV_LORA], kvg_ref[...]).astype(BF16)
    rc, rs1, rs2 = rc_ref[...], rs1_ref[...], rs2_ref[...]

    def rope(t):
        return t * rc + pltpu.roll(t, 16, 1) * rs1 + pltpu.roll(t, LANE - 16, 1) * rs2

    kr = rope(proj[:, C_KR:C_KR + LANE])
    tm = xb.shape[0]
    for h in range(MLA_HEADS):
        qh = jnp.dot(qn, wq_ref[h], preferred_element_type=F32)
        q_ref[0, h] = (rope(qh) * (MLA_SCALE * LOG2E)).astype(BF16)
        kh = jnp.dot(ckv, wk_ref[h], preferred_element_type=F32) + kr
        k_ref[0, h] = kh.astype(BF16)
        vth = lax.dot_general(wv_ref[h], ckv, NT_DIMS, preferred_element_type=F32)
        for j in range(tm // TQ_MLA):
            vt_ref[0, h, j] = vth[:, TQ_MLA * j:TQ_MLA * (j + 1)].astype(BF16)
    for h in range(DSA_HEADS):
        dqh = proj[:, C_DQ + DSA_DIM * h:C_DQ + DSA_DIM * (h + 1)] * (DSA_SCALE * LOG2E)
        feat = jnp.broadcast_to(qfeat_ref[h:h + 1, :], (tm, LANE - DSA_DIM))
        dq_ref[0, h] = jnp.concatenate([dqh, feat], axis=1).astype(BF16)
    for h in range(IDX_HEADS):
        iq_ref[0, h] = proj[:, C_IQ + IDX_DIM * h:C_IQ + IDX_DIM * (h + 1)].astype(BF16)
    dk_ref[0] = jnp.concatenate([proj[:, C_KX:C_KX + DSA_DIM], kfeat_ref[...]], axis=1).astype(BF16)
    ik_ref[0] = proj[:, C_KX + DSA_DIM:C_KX + DSA_DIM + IDX_DIM].astype(BF16)
    for j in range(TM_PROJ // TK_DSA):
        dvt_ref[0, j] = side[R_DV:R_DV + DSA_DIM, TK_DSA * j:TK_DSA * (j + 1)].astype(BF16)
    iwt_ref[0] = side[R_IW:R_IW + IDX_HEADS, :] * IDX_SCALE


def _proj_call(x, wcat, wside, qg, kvg, wq, wk, wv, rc, rs1, rs2, qfeat, kfeat):
    B, T, _ = x.shape
    tm = TM_PROJ
    full = lambda shape: pl.BlockSpec(shape, lambda b, t: (0,) * len(shape))
    head_out = lambda d: pl.BlockSpec((1, MLA_HEADS, tm, d), lambda b, t: (b, 0, t, 0))
    tok_out = lambda d: pl.BlockSpec((1, tm, d), lambda b, t: (b, t, 0))
    tab = pl.BlockSpec((tm, LANE), lambda b, t: (t, 0))
    return pl.pallas_call(
        _proj_kernel,
        grid=(B, T // tm),
        in_specs=[
            pl.BlockSpec((1, tm, D_MODEL), lambda b, t: (b, t, 0)),
            full(wcat.shape), full(wside.shape), full(qg.shape), full(kvg.shape), full(wq.shape), full(wk.shape),
            full(wv.shape), tab, tab, tab, full(qfeat.shape),
            pl.BlockSpec((tm, LANE - DSA_DIM), lambda b, t: (t, 0)),
        ],
        out_specs=[head_out(HEAD_PAD), head_out(HEAD_PAD),
                   pl.BlockSpec((1, MLA_HEADS, tm // TQ_MLA, MLA_V, TQ_MLA), lambda b, t: (b, 0, t, 0, 0)),
                   head_out(LANE),
                   tok_out(LANE),
                   pl.BlockSpec((1, tm // TK_DSA, DSA_DIM, TK_DSA), lambda b, t: (b, t, 0, 0)),
                   head_out(IDX_DIM), tok_out(IDX_DIM),
                   pl.BlockSpec((1, IDX_HEADS, tm), lambda b, t: (b, 0, t))],
        out_shape=[
            jax.ShapeDtypeStruct((B, MLA_HEADS, T, HEAD_PAD), BF16),
            jax.ShapeDtypeStruct((B, MLA_HEADS, T, HEAD_PAD), BF16),
            jax.ShapeDtypeStruct((B, MLA_HEADS, T // TQ_MLA, MLA_V, TQ_MLA), BF16),
            jax.ShapeDtypeStruct((B, DSA_HEADS, T, LANE), BF16),
            jax.ShapeDtypeStruct((B, T, LANE), BF16),
            jax.ShapeDtypeStruct((B, T // TK_DSA, DSA_DIM, TK_DSA), BF16),
            jax.ShapeDtypeStruct((B, IDX_HEADS, T, IDX_DIM), BF16),
            jax.ShapeDtypeStruct((B, T, IDX_DIM), BF16),
            jax.ShapeDtypeStruct((B, IDX_HEADS, T), F32),
        ],
        compiler_params=pltpu.CompilerParams(dimension_semantics=("parallel", "parallel")),
        name="proj",
    )(x, wcat, wside, qg, kvg, wq, wk, wv, rc, rs1, rs2, qfeat, kfeat)


def _transposed_heads_store(o_ref, ot, heads, width, tq):
    zpad = jnp.zeros((LANE - width, tq), F32)
    for h in range(heads):
        oh = jnp.concatenate([ot[:, tq * h:tq * (h + 1)], zpad], axis=0).T
        o_ref[0, :, width * h:width * (h + 1)] = oh[:, 0:width].astype(BF16)


def _mla_kernel(q_ref, k_ref, vt_ref, o_ref, m_ref, l_ref, acc_ref, base_ref):
    i = pl.program_id(1)
    tq = TQ_MLA
    heads = MLA_HEADS
    m_ref[...] = jnp.full(m_ref.shape, NEG, F32)
    l_ref[...] = jnp.zeros(l_ref.shape, F32)
    acc_ref[...] = jnp.zeros(acc_ref.shape, F32)
    key = lax.broadcasted_iota(I32, (tq, tq), 0)
    qpos = lax.broadcasted_iota(I32, (tq, tq), 1)
    base_ref[...] = jnp.where(key <= qpos, -NEG, NEG)

    def step(j, masked):
        start = pl.multiple_of(j * tq, tq)
        s_list = [lax.dot_general(k_ref[0, h, pl.ds(start, tq), :], q_ref[0, h], NT_DIMS,
                                  preferred_element_type=F32) for h in range(heads)]
        m_all, l_all, acc_all = m_ref[...], l_ref[...], acc_ref[...]
        m_out, l_out, a_out, pv_out = [], [], [], []
        for h in range(heads):
            lanes = slice(tq * h, tq * (h + 1))
            s = jnp.minimum(s_list[h], base_ref[...]) if masked else s_list[h]
            m_prev = m_all[:, lanes]
            m_new = jnp.maximum(m_prev, jnp.max(s, axis=0, keepdims=True))
            p = jnp.exp2(s - m_new)
            a = jnp.exp2(m_prev - m_new)
            l_out.append(a * l_all[:, lanes] + jnp.sum(p, axis=0, keepdims=True))
            pv_out.append(jnp.dot(vt_ref[0, h, j], p.astype(BF16), preferred_element_type=F32))
            a_out.append(a)
            m_out.append(m_new)
        m_ref[...] = jnp.concatenate(m_out, axis=1)
        l_ref[...] = jnp.concatenate(l_out, axis=1)
        acc_ref[...] = jnp.concatenate(a_out, axis=1) * acc_all + jnp.concatenate(pv_out, axis=1)

    def body_pair(j, carry):
        step(2 * j, False)
        step(2 * j + 1, False)
        return carry

    def body_odd_tail(j, carry):
        step(i - 1, False)
        step(i, True)
        return carry

    def body_even_tail(j, carry):
        step(i, True)
        return carry

    odd = i & 1
    lax.fori_loop(0, lax.shift_right_logical(i, 1), body_pair, 0)
    lax.fori_loop(0, odd, body_odd_tail, 0)
    lax.fori_loop(0, 1 - odd, body_even_tail, 0)
    _transposed_heads_store(o_ref, acc_ref[...] / l_ref[...], heads, MLA_V, tq)


def _mla_call(q, k, vt):
    B, H, T, _ = q.shape
    tq = TQ_MLA
    return pl.pallas_call(
        _mla_kernel,
        grid=(B, T // tq),
        in_specs=[
            pl.BlockSpec((1, H, tq, HEAD_PAD), lambda b, i: (b, 0, i, 0)),
            pl.BlockSpec((1, H, T, HEAD_PAD), lambda b, i: (b, 0, 0, 0)),
            pl.BlockSpec((1, H, T // tq, MLA_V, tq), lambda b, i: (b, 0, 0, 0, 0)),
        ],
        out_specs=pl.BlockSpec((1, tq, H * MLA_V), lambda b, i: (b, i, 0)),
        out_shape=jax.ShapeDtypeStruct((B, T, H * MLA_V), BF16),
        scratch_shapes=[
            pltpu.VMEM((1, H * tq), F32),
            pltpu.VMEM((1, H * tq), F32),
            pltpu.VMEM((MLA_V, H * tq), F32),
            pltpu.VMEM((tq, tq), F32),
        ],
        compiler_params=pltpu.CompilerParams(dimension_semantics=("parallel", "arbitrary")),
        name="mla",
    )(q, k, vt)


def _dsa_kernel(dq_ref, iq_ref, iwt_ref, dk_ref, dvt_ref, ik_ref, o_ref, sc_ref, m_ref, l_ref, acc_ref):
    i = pl.program_id(1)
    tq, tk = TQ_DSA, TK_DSA
    T = dk_ref.shape[1]
    nch = lax.div((i + 1) * tq + (tk - 1), tk)
    cols = DSA_HEADS * tq
    key = lax.broadcasted_iota(I32, (tk, tq), 0)
    t_q = i * tq + lax.broadcasted_iota(I32, (tk, tq), 1)
    t_row = i * tq + lax.broadcasted_iota(I32, (1, tq), 1)

    iq2 = iq_ref[0].reshape(cols, IDX_DIM)
    w8 = iwt_ref[0]

    def for_chunk_pairs(step):
        def pair_body(j, carry):
            step(2 * j)
            step(2 * j + 1)
            return carry

        lax.fori_loop(0, lax.shift_right_logical(nch, 1), pair_body, 0)

        @pl.when((nch & 1) == 1)
        def _():
            step(nch - 1)

    def score_step(c):
        start = pl.multiple_of(c * tk, tk)
        ikc = ik_ref[0, pl.ds(start, tk), :]
        rel = lax.dot_general(ikc, iq2, NT_DIMS, preferred_element_type=F32)
        sc = jnp.maximum(rel[:, 0:tq], 0.0) * w8[0:1]
        for h in range(1, IDX_HEADS):
            sc = sc + jnp.maximum(rel[:, tq * h:tq * (h + 1)], 0.0) * w8[h:h + 1]
        sc_ref[c] = jnp.where(key + c * tk <= t_q, sc, -jnp.inf)

    for_chunk_pairs(score_step)

    def count(pred):
        def hits(c):
            hit = jnp.where(pred(sc_ref[c], key + c * tk), 1.0, 0.0)
            return jnp.sum(hit.reshape(tk // SUBLANES, SUBLANES, tq), axis=0)

        def pair_body(j, acc):
            return acc + hits(2 * j) + hits(2 * j + 1)

        acc = lax.fori_loop(0, lax.shift_right_logical(nch, 1), pair_body, jnp.zeros((SUBLANES, tq), F32))
        acc = acc + lax.cond((nch & 1) == 1, lambda: hits(nch - 1), lambda: jnp.zeros((SUBLANES, tq), F32))
        return jnp.sum(acc, axis=0, keepdims=True)

    int_min = jnp.int32(-2 ** 31)

    def key_to_f32(ku):
        ks = ku ^ int_min
        bits = jnp.where(ks >= 0, ks, ks ^ jnp.int32(0x7FFFFFFF))
        return lax.bitcast_convert_type(bits, F32)

    kf = float(TOPK_MAX)

    def bis_body(step, carry):
        res, cnt_res = carry
        cand = res | lax.shift_left(jnp.int32(1), 31 - step)
        tau_c = key_to_f32(cand)
        cnt = count(lambda sc, idx: sc >= tau_c)
        ok = cnt >= kf
        return jnp.where(ok, cand, res), jnp.where(ok, cnt, cnt_res)

    bis_init = (jnp.zeros((1, tq), I32), jnp.full((1, tq), kf, F32))
    short = t_row < TOPK_MAX
    res, cnt_ge = lax.cond((i + 1) * tq > TOPK_MAX, lambda: lax.fori_loop(0, 32, bis_body, bis_init),
                           lambda: bis_init)
    tau = jnp.where(short, -jnp.inf, key_to_f32(res))
    cnt_ge = jnp.where(short, kf, cnt_ge)

    def tie_search():
        need = float(TOPK_MAX) - count(lambda sc, idx: sc > tau)

        def tie_body(step, jm):
            cand = jm | lax.shift_left(jnp.int32(1), 10 - step)
            cnt = count(lambda sc, idx: (sc >= tau) & jnp.logical_not(sc > tau) & (idx < cand))
            return jnp.where(cnt < need, cand, jm)

        return lax.fori_loop(0, 11, tie_body, jnp.zeros((1, tq), I32))

    j_max = lax.cond(jnp.max(cnt_ge) > float(TOPK_MAX), tie_search, lambda: jnp.full((1, tq), T, I32))

    q2 = dq_ref[0].reshape(cols, LANE)
    m_ref[...] = jnp.full((1, cols), NEG, F32)
    l_ref[...] = jnp.zeros((1, cols), F32)
    acc_ref[...] = jnp.zeros((DSA_DIM, cols), F32)

    def att_step(c):
        start = pl.multiple_of(c * tk, tk)
        kc = dk_ref[0, pl.ds(start, tk), :]
        vt = dvt_ref[0, c]
        sc = sc_ref[c]
        idx = key + c * tk
        sel = (sc >= tau) & ((sc > tau) | (idx <= j_max)) & (sc > -jnp.inf)
        cap = jnp.where(sel, -NEG, NEG)
        m_all, l_all, acc_all = m_ref[...], l_ref[...], acc_ref[...]
        m_out, l_out, a_out, p_out = [], [], [], []
        s_all = lax.dot_general(kc, q2, NT_DIMS, preferred_element_type=F32)
        for h in range(DSA_HEADS):
            lanes = slice(tq * h, tq * (h + 1))
            s = jnp.minimum(s_all[:, lanes], cap)
            m_prev = m_all[:, lanes]
            m_new = jnp.maximum(m_prev, jnp.max(s, axis=0, keepdims=True))
            p = jnp.exp2(s - m_new)
            a = jnp.exp2(m_prev - m_new)
            l_out.append(a * l_all[:, lanes] + jnp.sum(p, axis=0, keepdims=True))
            p_out.append(p.astype(BF16))
            a_out.append(a)
            m_out.append(m_new)
        pv = jnp.dot(vt, jnp.concatenate(p_out, axis=1), preferred_element_type=F32)
        m_ref[...] = jnp.concatenate(m_out, axis=1)
        l_ref[...] = jnp.concatenate(l_out, axis=1)
        acc_ref[...] = jnp.concatenate(a_out, axis=1) * acc_all + pv

    for_chunk_pairs(att_step)
    _transposed_heads_store(o_ref, acc_ref[...] / l_ref[...], DSA_HEADS, DSA_DIM, tq)


def _dsa_call(dq, iq, iwt, dk, dvt, ik):
    B, H, T, _ = dq.shape
    tq, tk = TQ_DSA, TK_DSA
    cols = H * tq
    return pl.pallas_call(
        _dsa_kernel,
        grid=(B, T // tq),
        in_specs=[
            pl.BlockSpec((1, H, tq, LANE), lambda b, i: (b, 0, i, 0)),
            pl.BlockSpec((1, H, tq, IDX_DIM), lambda b, i: (b, 0, i, 0)),
            pl.BlockSpec((1, IDX_HEADS, tq), lambda b, i: (b, 0, i)),
            pl.BlockSpec((1, T, LANE), lambda b, i: (b, 0, 0)),
            pl.BlockSpec((1, T // tk, DSA_DIM, tk), lambda b, i: (b, 0, 0, 0)),
            pl.BlockSpec((1, T, IDX_DIM), lambda b, i: (b, 0, 0)),
        ],
        out_specs=pl.BlockSpec((1, tq, H * DSA_DIM), lambda b, i: (b, i, 0)),
        out_shape=jax.ShapeDtypeStruct((B, T, H * DSA_DIM), BF16),
        scratch_shapes=[
            pltpu.VMEM((T // tk, tk, tq), F32),
            pltpu.VMEM((1, cols), F32),
            pltpu.VMEM((1, cols), F32),
            pltpu.VMEM((DSA_DIM, cols), F32),
        ],
        compiler_params=pltpu.CompilerParams(dimension_semantics=("parallel", "arbitrary")),
        name="dsa",
    )(dq, iq, iwt, dk, dvt, ik)


def _route(aff, biased):
    b = [biased[e:e + 1] for e in range(N_EXPERTS)]
    a = [aff[e:e + 1] for e in range(N_EXPERTS)]
    gs = []
    for g in range(N_GROUPS):
        v0, v1, v2, v3 = b[4 * g:4 * g + 4]
        hi1, lo1 = jnp.maximum(v0, v1), jnp.minimum(v0, v1)
        hi2, lo2 = jnp.maximum(v2, v3), jnp.minimum(v2, v3)
        gs.append(jnp.maximum(hi1, hi2) + jnp.maximum(jnp.minimum(hi1, hi2), jnp.maximum(lo1, lo2)))
    best, sel = gs[0], jnp.zeros_like(gs[0], dtype=I32)
    for g in range(1, N_GROUPS):
        upd = gs[g] > best
        sel = jnp.where(upd, g, sel)
        best = jnp.where(upd, gs[g], best)
    v, av = [], []
    for k in range(EXPERTS_PER_GROUP):
        vk, ak = b[k], a[k]
        for g in range(1, N_GROUPS):
            vk = jnp.where(sel == g, b[4 * g + k], vk)
            ak = jnp.where(sel == g, a[4 * g + k], ak)
        v.append(vk)
        av.append(ak)
    m1, i1 = v[0], jnp.zeros_like(sel)
    for k in range(1, EXPERTS_PER_GROUP):
        upd = v[k] > m1
        i1 = jnp.where(upd, k, i1)
        m1 = jnp.where(upd, v[k], m1)
    m2, i2 = jnp.full_like(m1, -jnp.inf), jnp.zeros_like(sel)
    for k in range(EXPERTS_PER_GROUP):
        cand = jnp.where(i1 == k, -jnp.inf, v[k])
        upd = cand > m2
        i2 = jnp.where(upd, k, i2)
        m2 = jnp.where(upd, cand, m2)
    a1, a2 = jnp.zeros_like(m1), jnp.zeros_like(m1)
    for k in range(EXPERTS_PER_GROUP):
        a1 = jnp.where(i1 == k, av[k], a1)
        a2 = jnp.where(i2 == k, av[k], a2)
    tot = a1 + a2
    e1 = sel * EXPERTS_PER_GROUP + i1
    e2 = sel * EXPERTS_PER_GROUP + i2
    return e1, e2, a1 / tot, a2 / tot


def _out_kernel(a_ref, b_ref, x_ref, wo_ref, g_ref, beta_ref, rwh_ref, rwl_ref, rb_ref, x1_ref, route_ref):
    half = MLA_HEADS * MLA_V
    mix = jnp.dot(a_ref[0], wo_ref[0:half, :], preferred_element_type=F32)
    mix = mix + jnp.dot(b_ref[0], wo_ref[half:2 * half, :], preferred_element_type=F32)
    x1 = _layer_norm(ALPHA * x_ref[0] + mix, g_ref[...], beta_ref[...])
    x1_ref[0] = x1
    xh = x1.astype(BF16)
    xl = (x1 - xh.astype(F32)).astype(BF16)
    rwh, rwl = rwh_ref[...], rwl_ref[...]
    logits = lax.dot_general(rwh, xh, NT_DIMS, preferred_element_type=F32)
    logits = logits + lax.dot_general(rwh, xl, NT_DIMS, preferred_element_type=F32)
    logits = logits + lax.dot_general(rwl, xh, NT_DIMS, preferred_element_type=F32)
    aff = 1.0 / (1.0 + jnp.exp(-logits))
    e1, e2, w1, w2 = _route(aff, aff + rb_ref[...])
    tm = logits.shape[1]
    rows = [e1.astype(F32), e2.astype(F32), w1, w2, jnp.zeros((LANE - 4, tm), F32)]
    route_ref[0] = jnp.concatenate(rows, axis=0).T


def _out_call(a, b, x, wo, g, beta, rwh, rwl, rb):
    B, T, _ = x.shape
    tm = TM_PROJ
    full = lambda shape: pl.BlockSpec(shape, lambda bb, t: (0,) * len(shape))
    tok = lambda d: pl.BlockSpec((1, tm, d), lambda bb, t: (bb, t, 0))
    return pl.pallas_call(
        _out_kernel,
        grid=(B, T // tm),
        in_specs=[tok(a.shape[-1]), tok(b.shape[-1]), tok(D_MODEL), full(wo.shape), full(g.shape), full(beta.shape),
                  full(rwh.shape), full(rwl.shape), full(rb.shape)],
        out_specs=[tok(D_MODEL), tok(LANE)],
        out_shape=[jax.ShapeDtypeStruct((B, T, D_MODEL), F32), jax.ShapeDtypeStruct((B, T, LANE), F32)],
        compiler_params=pltpu.CompilerParams(dimension_semantics=("parallel", "parallel")),
        name="outproj",
    )(a, b, x, wo, g, beta, rwh, rwl, rb)


def _route_positions(route, n_tiles):
    eid = route[:, 0:TOP_K].astype(I32).reshape(-1)
    onehot = (eid[:, None] == jnp.arange(N_EXPERTS, dtype=I32)[None, :]).astype(I32)
    csum = jnp.cumsum(onehot, axis=0)
    rank = jnp.sum(onehot * csum, axis=1) - 1
    counts = csum[-1]
    padded = ((counts + TR_MOE - 1) // TR_MOE) * TR_MOE
    ends = jnp.cumsum(padded)
    starts = ends - padded
    pos = jnp.sum(onehot * starts[None, :], axis=1) + rank
    n_active = ends[-1] // TR_MOE
    tile_row = jnp.minimum(jnp.arange(n_tiles, dtype=I32), n_active - 1) * TR_MOE
    tile_expert = jnp.sum((tile_row[:, None] >= ends[None, :]).astype(I32), axis=1)
    return pos.astype(I32), tile_expert.astype(I32), n_active.reshape(1).astype(I32)


def _row_copy_loop(tm, copy):
    def body(q, carry):
        base = pl.multiple_of(q * SUBLANES, SUBLANES)
        for j in range(SUBLANES):
            for k in range(TOP_K):
                copy(base + j, (base + j) * TOP_K + k, k).start(priority=k)
        return carry

    lax.fori_loop(0, tm // SUBLANES, body, 0)


def _dispatch_kernel(pos_ref, x_ref, xs_in_ref, xs_ref, xr_ref, sem):
    del xs_in_ref
    tm = x_ref.shape[0]
    xr_ref[...] = x_ref[...].reshape(tm, SUBLANES, LANE)
    _row_copy_loop(tm, lambda r, p, k: pltpu.make_async_copy(xr_ref.at[r], xs_ref.at[pos_ref[p]], sem))
    for k in range(TOP_K):
        pltpu.make_async_copy(xr_ref, xs_ref.at[pl.ds(0, tm)], sem).wait()


def _dispatch_call(pos, x1, xs_zero):
    n = x1.shape[0]
    tm = TM_DISP
    return pl.pallas_call(
        _dispatch_kernel,
        grid=(n // tm,),
        in_specs=[
            pl.BlockSpec((TOP_K * tm,), lambda i: (i,), memory_space=pltpu.SMEM),
            pl.BlockSpec((tm, D_MODEL), lambda i: (i, 0)),
            pl.BlockSpec(memory_space=pl.ANY),
        ],
        out_specs=pl.BlockSpec(memory_space=pl.ANY),
        out_shape=jax.ShapeDtypeStruct(xs_zero.shape, xs_zero.dtype),
        scratch_shapes=[pltpu.VMEM((tm, SUBLANES, LANE), F32), pltpu.SemaphoreType.DMA(())],
        input_output_aliases={2: 0},
        compiler_params=pltpu.CompilerParams(dimension_semantics=("arbitrary",)),
        name="dispatch",
    )(pos, x1, xs_zero)


def _expert_kernel(te_ref, na_ref, xs_ref, wg_ref, wu_ref, wd_ref, ys_ref):
    del te_ref
    active = pl.program_id(0) < na_ref[0]

    @pl.when(active)
    def _():
        tr = xs_ref.shape[0]
        xb = xs_ref[...].reshape(tr, D_MODEL).astype(BF16)
        hg = jnp.dot(xb, wg_ref[0, 0].astype(BF16), preferred_element_type=F32)
        hu = jnp.dot(xb, wu_ref[0, 0].astype(BF16), preferred_element_type=F32)
        act = (hg / (1.0 + jnp.exp(-hg))) * hu
        y = jnp.dot(act.astype(BF16), wd_ref[0, 0].astype(BF16), preferred_element_type=F32)
        ys_ref[...] = y.reshape(tr, SUBLANES, LANE)

    @pl.when(jnp.logical_not(active))
    def _():
        ys_ref[...] = jnp.zeros_like(ys_ref)


def _expert_call(tile_expert, n_active, xs, wg, wu, wd, layer):
    rows = xs.shape[0]
    tr = TR_MOE
    row_map = lambda i, te, na: (jnp.minimum(i, na[0] - 1), 0, 0)
    out_map = lambda i, te, na: (i, 0, 0)
    w_map = lambda i, te, na: (layer, te[i], 0, 0)
    return pl.pallas_call(
        _expert_kernel,
        grid_spec=pltpu.PrefetchScalarGridSpec(
            num_scalar_prefetch=2,
            grid=(rows // tr,),
            in_specs=[
                pl.BlockSpec((tr, SUBLANES, LANE), row_map),
                pl.BlockSpec((1, 1, D_MODEL, D_FF), w_map),
                pl.BlockSpec((1, 1, D_MODEL, D_FF), w_map),
                pl.BlockSpec((1, 1, D_FF, D_MODEL), w_map),
            ],
            out_specs=pl.BlockSpec((tr, SUBLANES, LANE), out_map),
        ),
        out_shape=jax.ShapeDtypeStruct(xs.shape, F32),
        compiler_params=pltpu.CompilerParams(dimension_semantics=("arbitrary",)),
        name="experts",
    )(tile_expert, n_active, xs, wg, wu, wd)


def _combine_kernel(pos_ref, x1_ref, route_ref, ys_ref, g_ref, beta_ref, o_ref, buf_ref, sem):
    tm = x1_ref.shape[0]

    _row_copy_loop(tm, lambda r, p, k: pltpu.make_async_copy(ys_ref.at[pos_ref[p]], buf_ref.at[k, r], sem))
    for k in range(TOP_K):
        pltpu.make_async_copy(ys_ref.at[pl.ds(0, tm)], buf_ref.at[k], sem).wait()
    rt = route_ref[...]
    ffn = rt[:, TOP_K:TOP_K + 1] * buf_ref[0].reshape(tm, D_MODEL)
    for k in range(1, TOP_K):
        ffn = ffn + rt[:, TOP_K + k:TOP_K + k + 1] * buf_ref[k].reshape(tm, D_MODEL)
    o_ref[...] = _layer_norm(ALPHA * x1_ref[...] + ffn, g_ref[...], beta_ref[...])


def _combine_call(pos, x1, route, ys, g, beta):
    n = x1.shape[0]
    tm = TM_DISP
    vec = pl.BlockSpec((1, D_MODEL), lambda i: (0, 0))
    return pl.pallas_call(
        _combine_kernel,
        grid=(n // tm,),
        in_specs=[
            pl.BlockSpec((TOP_K * tm,), lambda i: (i,), memory_space=pltpu.SMEM),
            pl.BlockSpec((tm, D_MODEL), lambda i: (i, 0)),
            pl.BlockSpec((tm, LANE), lambda i: (i, 0)),
            pl.BlockSpec(memory_space=pl.ANY),
            vec, vec,
        ],
        out_specs=pl.BlockSpec((tm, D_MODEL), lambda i: (i, 0)),
        out_shape=jax.ShapeDtypeStruct((n, D_MODEL), F32),
        scratch_shapes=[pltpu.VMEM((TOP_K, tm, SUBLANES, LANE), F32), pltpu.SemaphoreType.DMA(())],
        compiler_params=pltpu.CompilerParams(dimension_semantics=("arbitrary",)),
        name="combine",
    )(pos, x1, route, ys, g, beta)


def _moe_call(x1, route, wg, wu, wd, layer, g, beta):
    n = x1.shape[0]
    n_tiles = TOP_K * n // TR_MOE + N_EXPERTS
    pos, tile_expert, n_active = _route_positions(route, n_tiles)
    xs = _dispatch_call(pos, x1, jnp.zeros((n_tiles * TR_MOE, SUBLANES, LANE), F32))
    ys = _expert_call(tile_expert, n_active, xs, wg, wu, wd, layer)
    return _combine_call(pos, x1, route, ys, g, beta)


def _rope_tables(T):
    half = MLA_ROPE // 2
    pos = jnp.arange(T, dtype=F32)
    inv = ROPE_BASE ** (-jnp.arange(half, dtype=F32) / half)
    ang = pos[:, None] * inv[None, :]
    cos, sin = jnp.cos(ang), jnp.sin(ang)
    one = jnp.ones((T, MLA_NOPE), F32)
    z16 = jnp.zeros((T, half), F32)
    z32 = jnp.zeros((T, LANE - MLA_NOPE - MLA_ROPE), F32)
    z64 = jnp.zeros((T, MLA_NOPE), F32)
    rc = jnp.concatenate([one, cos, cos, z32], axis=1)
    rs1 = jnp.concatenate([z64, z16, sin, z32], axis=1)
    rs2 = jnp.concatenate([z64, -sin, z16, z32], axis=1)
    return rc, rs1, rs2


def _alibi_features(T):
    c = jnp.asarray(SLOPES, F32) * LOG2E
    c1 = c.astype(BF16).astype(F32)
    c2 = (c - c1).astype(BF16).astype(F32)
    c3 = (c - c1 - c2).astype(BF16).astype(F32)
    width = LANE - DSA_DIM
    qfeat = jnp.stack([c1, c1, c2, c2, c3, c3], axis=1)
    qfeat = jnp.pad(qfeat, ((0, 0), (0, width - qfeat.shape[1])))
    pos = jnp.arange(T, dtype=I32)
    hi = ((pos // 256) * 256).astype(F32)
    lo = (pos % 256).astype(F32)
    kfeat = jnp.stack([hi, lo, hi, lo, hi, lo], axis=1)
    kfeat = jnp.pad(kfeat, ((0, 0), (0, width - kfeat.shape[1])))
    return qfeat, kfeat


def _pack_w_in(w):
    sizes = (Q_LORA, KV_LORA, MLA_ROPE, DSA_HEADS * DSA_DIM, DSA_DIM, DSA_DIM, IDX_HEADS * IDX_DIM, IDX_DIM, IDX_HEADS)
    offs = np.concatenate([[0], np.cumsum(sizes)])
    w = w.astype(BF16)
    qa, kva, kr, dq, dk, dv, iq, ik, iw = [w[:, int(offs[j]):int(offs[j + 1])] for j in range(len(sizes))]
    z = lambda n: jnp.zeros((w.shape[0], n), w.dtype)
    cat = jnp.concatenate([
        qa, kva, dq, iq,
        z(MLA_NOPE), kr, z(LANE - MLA_NOPE - MLA_ROPE),
        dk, ik, z(LANE - DSA_DIM - IDX_DIM),
    ], axis=1)
    assert cat.shape[1] == C_END
    side = jnp.concatenate([dv, iw, z(R_END - DSA_DIM - IDX_HEADS)], axis=1).T
    return cat, side


def kernel(x, w_in, q_norm_g, w_q_up, kv_norm_g, w_uk, w_uv, w_o, ln1_g, ln1_b, router_w, router_bias,
           w_gate, w_up, w_down, ln2_g, ln2_b):
    B, T, D = x.shape
    rc, rs1, rs2 = _rope_tables(T)
    rwt = router_w.T
    rwh = rwt.astype(BF16)
    rwl = (rwt - rwh.astype(F32)).astype(BF16)
    rb = router_bias.reshape(N_EXPERTS, 1).astype(F32)
    qfeat, kfeat = _alibi_features(T)
    for l in range(DEPTH):
        wcat, wside = _pack_w_in(w_in[l])
        wq = w_q_up[l].reshape(Q_LORA, MLA_HEADS, MLA_NOPE + MLA_ROPE).transpose(1, 0, 2)
        wq = jnp.pad(wq, ((0, 0), (0, 0), (0, HEAD_PAD - MLA_NOPE - MLA_ROPE))).astype(BF16)
        wk = w_uk[l].reshape(KV_LORA, MLA_HEADS, MLA_NOPE).transpose(1, 0, 2)
        wk = jnp.pad(wk, ((0, 0), (0, 0), (0, HEAD_PAD - MLA_NOPE))).astype(BF16)
        wvt = w_uv[l].reshape(KV_LORA, MLA_HEADS, MLA_V).transpose(1, 2, 0).astype(BF16)
        q, k, vt, dq, dk, dvt, iq, ik, iwt = _proj_call(
            x, wcat, wside, q_norm_g[l].reshape(1, Q_LORA), kv_norm_g[l].reshape(1, KV_LORA), wq, wk, wvt,
            rc, rs1, rs2, qfeat, kfeat)
        out_a = _mla_call(q, k, vt)
        out_b = _dsa_call(dq, iq, iwt, dk, dvt, ik)
        x1, route = _out_call(out_a, out_b, x, w_o[l].astype(BF16), ln1_g[l].reshape(1, D), ln1_b[l].reshape(1, D),
                              rwh, rwl, rb)
        y = _moe_call(x1.reshape(B * T, D), route.reshape(B * T, LANE), w_gate, w_up,
                      w_down, l, ln2_g[l].reshape(1, D), ln2_b[l].reshape(1, D))
        x = y.reshape(B, T, D)
    return x
```

```python
import jax
import jax.numpy as jnp
import numpy as np
from jax import lax
from jax.experimental import pallas as pl
from jax.experimental.pallas import tpu as pltpu

F32 = jnp.float32
BF16 = jnp.bfloat16
I32 = jnp.int32

D_MODEL = 1024
DEPTH = 2
MLA_HEADS = 8
MLA_NOPE = 64
MLA_ROPE = 32
MLA_V = 64
Q_LORA = 384
KV_LORA = 256
ROPE_BASE = 10000.0
DSA_HEADS = 8
DSA_DIM = 64
IDX_HEADS = 8
IDX_DIM = 32
TOPK_MAX = 256
N_EXPERTS = 16
N_GROUPS = 4
EXPERTS_PER_GROUP = 4
D_FF = 512
ALPHA = (2.0 * DEPTH) ** 0.25
LN_EPS = 1e-5
RMS_EPS = 1e-6
MLA_SCALE = (MLA_NOPE + MLA_ROPE) ** -0.5
LOG2E = 1.4426950408889634
SLOPES = tuple(2.0 ** (-8.0 * (h + 1) / DSA_HEADS) for h in range(DSA_HEADS))

LANE = 128
SUBLANES = 8
HEAD_PAD = LANE
NEG = -1e30

C_QA = 0
C_KVA = C_QA + Q_LORA
C_DQ = C_KVA + KV_LORA
C_IQ = C_DQ + DSA_HEADS * DSA_DIM
C_KR = C_IQ + IDX_HEADS * IDX_DIM
C_KX = C_KR + LANE
C_END = C_KX + LANE
R_DV = 0
R_IW = DSA_DIM
R_END = LANE
IDX_SCALE = (IDX_DIM * IDX_HEADS) ** -0.5
DSA_SCALE = DSA_DIM ** -0.5

TM_PROJ = 512
TQ_MLA = 256
TQ_DSA = 256
TK_DSA = 256
TOP_K = 2
TR_MOE = 512
TM_DISP = 512

NT_DIMS = (((1,), (1,)), ((), ()))


def _rms(x, g):
    return x * lax.rsqrt(jnp.mean(x * x, axis=-1, keepdims=True) + RMS_EPS) * g


def _layer_norm(y, g, b):
    mu = jnp.mean(y, axis=-1, keepdims=True)
    d = y - mu
    var = jnp.mean(d * d, axis=-1, keepdims=True)
    return d * lax.rsqrt(var + LN_EPS) * g + b


def _proj_kernel(x_ref, wcat_ref, wside_ref, qg_ref, kvg_ref, wq_ref, wk_ref, wv_ref, rc_ref, rs1_ref, rs2_ref,
                 qfeat_ref, kfeat_ref,
                 q_ref, k_ref, vt_ref, dq_ref, dk_ref, dvt_ref, iq_ref, ik_ref, iwt_ref):
    xb = x_ref[0].astype(BF16)
    proj = jnp.dot(xb, wcat_ref[...], preferred_element_type=F32)
    side = lax.dot_general(wside_ref[...], xb, NT_DIMS, preferred_element_type=F32)
    qn = _rms(proj[:, C_QA:C_QA + Q_LORA], qg_ref[...]).astype(BF16)
    ckv = _rms(proj[:, C_KVA:C_KVA + KV_LORA], kvg_ref[...]).astype(BF16)
    rc, rs1, rs2 = rc_ref[...], rs1_ref[...], rs2_ref[...]

    def rope(t):
        return t * rc + pltpu.roll(t, 16, 1) * rs1 + pltpu.roll(t, LANE - 16, 1) * rs2

    kr = rope(proj[:, C_KR:C_KR + LANE])
    tm = xb.shape[0]
    for h in range(MLA_HEADS):
        qh = jnp.dot(qn, wq_ref[h], preferred_element_type=F32)
        q_ref[0, h] = (rope(qh) * (MLA_SCALE * LOG2E)).astype(BF16)
        kh = jnp.dot(ckv, wk_ref[h], preferred_element_type=F32) + kr
        k_ref[0, h] = kh.astype(BF16)
        vth = lax.dot_general(wv_ref[h], ckv, NT_DIMS, preferred_element_type=F32)
        for j in range(tm // TQ_MLA):
            vt_ref[0, h, j] = vth[:, TQ_MLA * j:TQ_MLA * (j + 1)].astype(BF16)
    for h in range(DSA_HEADS):
        dqh = proj[:, C_DQ + DSA_DIM * h:C_DQ + DSA_DIM * (h + 1)] * (DSA_SCALE * LOG2E)
        feat = jnp.broadcast_to(qfeat_ref[h:h + 1, :], (tm, LANE - DSA_DIM))
        dq_ref[0, h] = jnp.concatenate([dqh, feat], axis=1).astype(BF16)
    for h in range(IDX_HEADS):
        iq_ref[0, h] = proj[:, C_IQ + IDX_DIM * h:C_IQ + IDX_DIM * (h + 1)].astype(BF16)
    dk_ref[0] = jnp.concatenate([proj[:, C_KX:C_KX + DSA_DIM], kfeat_ref[...]], axis=1).astype(BF16)
    ik_ref[0] = proj[:, C_KX + DSA_DIM:C_KX + DSA_DIM + IDX_DIM].astype(BF16)
    for j in range(TM_PROJ // TK_DSA):
        dvt_ref[0, j] = side[R_DV:R_DV + DSA_DIM, TK_DSA * j:TK_DSA * (j + 1)].astype(BF16)
    iwt_ref[0] = side[R_IW:R_IW + IDX_HEADS, :] * IDX_SCALE


def _proj_call(x, wcat, wside, qg, kvg, wq, wk, wv, rc, rs1, rs2, qfeat, kfeat):
    B, T, _ = x.shape
    tm = TM_PROJ
    full = lambda shape: pl.BlockSpec(shape, lambda b, t: (0,) * len(shape))
    head_out = lambda d: pl.BlockSpec((1, MLA_HEADS, tm, d), lambda b, t: (b, 0, t, 0))
    tok_out = lambda d: pl.BlockSpec((1, tm, d), lambda b, t: (b, t, 0))
    tab = pl.BlockSpec((tm, LANE), lambda b, t: (t, 0))
    return pl.pallas_call(
        _proj_kernel,
        grid=(B, T // tm),
        in_specs=[
            pl.BlockSpec((1, tm, D_MODEL), lambda b, t: (b, t, 0)),
            full(wcat.shape), full(wside.shape), full(qg.shape), full(kvg.shape), full(wq.shape), full(wk.shape),
            full(wv.shape), tab, tab, tab, full(qfeat.shape),
            pl.BlockSpec((tm, LANE - DSA_DIM), lambda b, t: (t, 0)),
        ],
        out_specs=[head_out(HEAD_PAD), head_out(HEAD_PAD),
                   pl.BlockSpec((1, MLA_HEADS, tm // TQ_MLA, MLA_V, TQ_MLA), lambda b, t: (b, 0, t, 0, 0)),
                   head_out(LANE),
                   tok_out(LANE),
                   pl.BlockSpec((1, tm // TK_DSA, DSA_DIM, TK_DSA), lambda b, t: (b, t, 0, 0)),
                   head_out(IDX_DIM), tok_out(IDX_DIM),
                   pl.BlockSpec((1, IDX_HEADS, tm), lambda b, t: (b, 0, t))],
        out_shape=[
            jax.ShapeDtypeStruct((B, MLA_HEADS, T, HEAD_PAD), BF16),
            jax.ShapeDtypeStruct((B, MLA_HEADS, T, HEAD_PAD), BF16),
            jax.ShapeDtypeStruct((B, MLA_HEADS, T // TQ_MLA, MLA_V, TQ_MLA), BF16),
            jax.ShapeDtypeStruct((B, DSA_HEADS, T, LANE), BF16),
            jax.ShapeDtypeStruct((B, T, LANE), BF16),
            jax.ShapeDtypeStruct((B, T // TK_DSA, DSA_DIM, TK_DSA), BF16),
            jax.ShapeDtypeStruct((B, IDX_HEADS, T, IDX_DIM), BF16),
            jax.ShapeDtypeStruct((B, T, IDX_DIM), BF16),
            jax.ShapeDtypeStruct((B, IDX_HEADS, T), F32),
        ],
        compiler_params=pltpu.CompilerParams(dimension_semantics=("parallel", "parallel")),
        name="proj",
    )(x, wcat, wside, qg, kvg, wq, wk, wv, rc, rs1, rs2, qfeat, kfeat)


def _transposed_heads_store(o_ref, ot, heads, width, tq):
    zpad = jnp.zeros((LANE - width, tq), F32)
    for h in range(heads):
        oh = jnp.concatenate([ot[:, tq * h:tq * (h + 1)], zpad], axis=0).T
        o_ref[0, :, width * h:width * (h + 1)] = oh[:, 0:width].astype(BF16)


def _mla_kernel(q_ref, k_ref, vt_ref, o_ref, m_ref, l_ref, acc_ref, base_ref):
    i = pl.program_id(1)
    tq = TQ_MLA
    heads = MLA_HEADS
    m_ref[...] = jnp.full(m_ref.shape, NEG, F32)
    l_ref[...] = jnp.zeros(l_ref.shape, F32)
    acc_ref[...] = jnp.zeros(acc_ref.shape, F32)
    key = lax.broadcasted_iota(I32, (tq, tq), 0)
    qpos = lax.broadcasted_iota(I32, (tq, tq), 1)
    base_ref[...] = jnp.where(key <= qpos, -NEG, NEG)

    def step(j, masked):
        start = pl.multiple_of(j * tq, tq)
        s_list = [lax.dot_general(k_ref[0, h, pl.ds(start, tq), :], q_ref[0, h], NT_DIMS,
                                  preferred_element_type=F32) for h in range(heads)]
        m_all, l_all, acc_all = m_ref[...], l_ref[...], acc_ref[...]
        m_out, l_out, a_out, pv_out = [], [], [], []
        for h in range(heads):
            lanes = slice(tq * h, tq * (h + 1))
            s = jnp.minimum(s_list[h], base_ref[...]) if masked else s_list[h]
            m_prev = m_all[:, lanes]
            m_new = jnp.maximum(m_prev, jnp.max(s, axis=0, keepdims=True))
            p = jnp.exp2(s - m_new)
            a = jnp.exp2(m_prev - m_new)
            l_out.append(a * l_all[:, lanes] + jnp.sum(p, axis=0, keepdims=True))
            pv_out.append(jnp.dot(vt_ref[0, h, j], p.astype(BF16), preferred_element_type=F32))
            a_out.append(a)
            m_out.append(m_new)
        m_ref[...] = jnp.concatenate(m_out, axis=1)
        l_ref[...] = jnp.concatenate(l_out, axis=1)
        acc_ref[...] = jnp.concatenate(a_out, axis=1) * acc_all + jnp.concatenate(pv_out, axis=1)

    def body_pair(j, carry):
        step(2 * j, False)
        step(2 * j + 1, False)
        return carry

    def body_odd_tail(j, carry):
        step(i - 1, False)
        step(i, True)
        return carry

    def body_even_tail(j, carry):
        step(i, True)
        return carry

    odd = i & 1
    lax.fori_loop(0, lax.shift_right_logical(i, 1), body_pair, 0)
    lax.fori_loop(0, odd, body_odd_tail, 0)
    lax.fori_loop(0, 1 - odd, body_even_tail, 0)
    _transposed_heads_store(o_ref, acc_ref[...] / l_ref[...], heads, MLA_V, tq)


def _mla_call(q, k, vt):
    B, H, T, _ = q.shape
    tq = TQ_MLA
    return pl.pallas_call(
        _mla_kernel,
        grid=(B, T // tq),
        in_specs=[
            pl.BlockSpec((1, H, tq, HEAD_PAD), lambda b, i: (b, 0, i, 0)),
            pl.BlockSpec((1, H, T, HEAD_PAD), lambda b, i: (b, 0, 0, 0)),
            pl.BlockSpec((1, H, T // tq, MLA_V, tq), lambda b, i: (b, 0, 0, 0, 0)),
        ],
        out_specs=pl.BlockSpec((1, tq, H * MLA_V), lambda b, i: (b, i, 0)),
        out_shape=jax.ShapeDtypeStruct((B, T, H * MLA_V), BF16),
        scratch_shapes=[
            pltpu.VMEM((1, H * tq), F32),
            pltpu.VMEM((1, H * tq), F32),
            pltpu.VMEM((MLA_V, H * tq), F32),
            pltpu.VMEM((tq, tq), F32),
        ],
        compiler_params=pltpu.CompilerParams(dimension_semantics=("parallel", "arbitrary")),
        name="mla",
    )(q, k, vt)


def _dsa_kernel(dq_ref, iq_ref, iwt_ref, dk_ref, dvt_ref, ik_ref, o_ref, sc_ref, m_ref, l_ref, acc_ref):
    i = pl.program_id(1)
    tq, tk = TQ_DSA, TK_DSA
    T = dk_ref.shape[1]
    nch = lax.div((i + 1) * tq + (tk - 1), tk)
    cols = DSA_HEADS * tq
    key = lax.broadcasted_iota(I32, (tk, tq), 0)
    t_q = i * tq + lax.broadcasted_iota(I32, (tk, tq), 1)
    t_row = i * tq + lax.broadcasted_iota(I32, (1, tq), 1)

    iq2 = iq_ref[0].reshape(cols, IDX_DIM)
    w8 = iwt_ref[0]

    def for_chunk_pairs(step):
        def pair_body(j, carry):
            step(2 * j)
            step(2 * j + 1)
            return carry

        lax.fori_loop(0, lax.shift_right_logical(nch, 1), pair_body, 0)

        @pl.when((nch & 1) == 1)
        def _():
            step(nch - 1)

    def score_step(c):
        start = pl.multiple_of(c * tk, tk)
        ikc = ik_ref[0, pl.ds(start, tk), :]
        rel = lax.dot_general(ikc, iq2, NT_DIMS, preferred_element_type=F32)
        sc = jnp.maximum(rel[:, 0:tq], 0.0) * w8[0:1]
        for h in range(1, IDX_HEADS):
            sc = sc + jnp.maximum(rel[:, tq * h:tq * (h + 1)], 0.0) * w8[h:h + 1]
        sc_ref[c] = jnp.where(key + c * tk <= t_q, sc, -jnp.inf)

    for_chunk_pairs(score_step)

    def count(pred):
        def hits(c):
            hit = jnp.where(pred(sc_ref[c], key + c * tk), 1.0, 0.0)
            return jnp.sum(hit.reshape(tk // SUBLANES, SUBLANES, tq), axis=0)

        def pair_body(j, acc):
            return acc + hits(2 * j) + hits(2 * j + 1)

        acc = lax.fori_loop(0, lax.shift_right_logical(nch, 1), pair_body, jnp.zeros((SUBLANES, tq), F32))
        acc = acc + lax.cond((nch & 1) == 1, lambda: hits(nch - 1), lambda: jnp.zeros((SUBLANES, tq), F32))
        return jnp.sum(acc, axis=0, keepdims=True)

    int_min = jnp.int32(-2 ** 31)

    def key_to_f32(ku):
        ks = ku ^ int_min
        bits = jnp.where(ks >= 0, ks, ks ^ jnp.int32(0x7FFFFFFF))
        return lax.bitcast_convert_type(bits, F32)

    kf = float(TOPK_MAX)

    def bis_body(step, carry):
        res, cnt_res = carry
        cand = res | lax.shift_left(jnp.int32(1), 31 - step)
        tau_c = key_to_f32(cand)
        cnt = count(lambda sc, idx: sc >= tau_c)
        ok = cnt >= kf
        return jnp.where(ok, cand, res), jnp.where(ok, cnt, cnt_res)

    bis_init = (jnp.zeros((1, tq), I32), jnp.full((1, tq), kf, F32))
    short = t_row < TOPK_MAX
    res, cnt_ge = lax.cond((i + 1) * tq > TOPK_MAX, lambda: lax.fori_loop(0, 32, bis_body, bis_init),
                           lambda: bis_init)
    tau = jnp.where(short, -jnp.inf, key_to_f32(res))
    cnt_ge = jnp.where(short, kf, cnt_ge)

    def tie_search():
        need = float(TOPK_MAX) - count(lambda sc, idx: sc > tau)

        def tie_body(step, jm):
            cand = jm | lax.shift_left(jnp.int32(1), 10 - step)
            cnt = count(lambda sc, idx: (sc >= tau) & jnp.logical_not(sc > tau) & (idx < cand))
            return jnp.where(cnt < need, cand, jm)

        return lax.fori_loop(0, 11, tie_body, jnp.zeros((1, tq), I32))

    j_max = lax.cond(jnp.max(cnt_ge) > float(TOPK_MAX), tie_search, lambda: jnp.full((1, tq), T, I32))

    q2 = dq_ref[0].reshape(cols, LANE)
    m_ref[...] = jnp.full((1, cols), NEG, F32)
    l_ref[...] = jnp.zeros((1, cols), F32)
    acc_ref[...] = jnp.zeros((DSA_DIM, cols), F32)

    def att_step(c):
        start = pl.multiple_of(c * tk, tk)
        kc = dk_ref[0, pl.ds(start, tk), :]
        vt = dvt_ref[0, c]
        sc = sc_ref[c]
        idx = key + c * tk
        sel = (sc >= tau) & ((sc > tau) | (idx <= j_max)) & (sc > -jnp.inf)
        cap = jnp.where(sel, -NEG, NEG)
        m_all, l_all, acc_all = m_ref[...], l_ref[...], acc_ref[...]
        m_out, l_out, a_out, p_out = [], [], [], []
        s_all = lax.dot_general(kc, q2, NT_DIMS, preferred_element_type=F32)
        for h in range(DSA_HEADS):
            lanes = slice(tq * h, tq * (h + 1))
            s = jnp.minimum(s_all[:, lanes], cap)
            m_prev = m_all[:, lanes]
            m_new = jnp.maximum(m_prev, jnp.max(s, axis=0, keepdims=True))
            p = jnp.exp2(s - m_new)
            a = jnp.exp2(m_prev - m_new)
            l_out.append(a * l_all[:, lanes] + jnp.sum(p, axis=0, keepdims=True))
            p_out.append(p.astype(BF16))
            a_out.append(a)
            m_out.append(m_new)
        pv = jnp.dot(vt, jnp.concatenate(p_out, axis=1), preferred_element_type=F32)
        m_ref[...] = jnp.concatenate(m_out, axis=1)
        l_ref[...] = jnp.concatenate(l_out, axis=1)
        acc_ref[...] = jnp.concatenate(a_out, axis=1) * acc_all + pv

    for_chunk_pairs(att_step)
    _transposed_heads_store(o_ref, acc_ref[...] / l_ref[...], DSA_HEADS, DSA_DIM, tq)


def _dsa_call(dq, iq, iwt, dk, dvt, ik):
    B, H, T, _ = dq.shape
    tq, tk = TQ_DSA, TK_DSA
    cols = H * tq
    return pl.pallas_call(
        _dsa_kernel,
        grid=(B, T // tq),
        in_specs=[
            pl.BlockSpec((1, H, tq, LANE), lambda b, i: (b, 0, i, 0)),
            pl.BlockSpec((1, H, tq, IDX_DIM), lambda b, i: (b, 0, i, 0)),
            pl.BlockSpec((1, IDX_HEADS, tq), lambda b, i: (b, 0, i)),
            pl.BlockSpec((1, T, LANE), lambda b, i: (b, 0, 0)),
            pl.BlockSpec((1, T // tk, DSA_DIM, tk), lambda b, i: (b, 0, 0, 0)),
            pl.BlockSpec((1, T, IDX_DIM), lambda b, i: (b, 0, 0)),
        ],
        out_specs=pl.BlockSpec((1, tq, H * DSA_DIM), lambda b, i: (b, i, 0)),
        out_shape=jax.ShapeDtypeStruct((B, T, H * DSA_DIM), BF16),
        scratch_shapes=[
            pltpu.VMEM((T // tk, tk, tq), F32),
            pltpu.VMEM((1, cols), F32),
            pltpu.VMEM((1, cols), F32),
            pltpu.VMEM((DSA_DIM, cols), F32),
        ],
        compiler_params=pltpu.CompilerParams(dimension_semantics=("parallel", "arbitrary")),
        name="dsa",
    )(dq, iq, iwt, dk, dvt, ik)


def _route(aff, biased):
    b = [biased[e:e + 1] for e in range(N_EXPERTS)]
    a = [aff[e:e + 1] for e in range(N_EXPERTS)]
    gs = []
    for g in range(N_GROUPS):
        v0, v1, v2, v3 = b[4 * g:4 * g + 4]
        hi1, lo1 = jnp.maximum(v0, v1), jnp.minimum(v0, v1)
        hi2, lo2 = jnp.maximum(v2, v3), jnp.minimum(v2, v3)
        gs.append(jnp.maximum(hi1, hi2) + jnp.maximum(jnp.minimum(hi1, hi2), jnp.maximum(lo1, lo2)))
    best, sel = gs[0], jnp.zeros_like(gs[0], dtype=I32)
    for g in range(1, N_GROUPS):
        upd = gs[g] > best
        sel = jnp.where(upd, g, sel)
        best = jnp.where(upd, gs[g], best)
    v, av = [], []
    for k in range(EXPERTS_PER_GROUP):
        vk, ak = b[k], a[k]
        for g in range(1, N_GROUPS):
            vk = jnp.where(sel == g, b[4 * g + k], vk)
            ak = jnp.where(sel == g, a[4 * g + k], ak)
        v.append(vk)
        av.append(ak)
    m1, i1 = v[0], jnp.zeros_like(sel)
    for k in range(1, EXPERTS_PER_GROUP):
        upd = v[k] > m1
        i1 = jnp.where(upd, k, i1)
        m1 = jnp.where(upd, v[k], m1)
    m2, i2 = jnp.full_like(m1, -jnp.inf), jnp.zeros_like(sel)
    for k in range(EXPERTS_PER_GROUP):
        cand = jnp.where(i1 == k, -jnp.inf, v[k])
        upd = cand > m2
        i2 = jnp.where(upd, k, i2)
        m2 = jnp.where(upd, cand, m2)
    a1, a2 = jnp.zeros_like(m1), jnp.zeros_like(m1)
    for k in range(EXPERTS_PER_GROUP):
        a1 = jnp.where(i1 == k, av[k], a1)
        a2 = jnp.where(i2 == k, av[k], a2)
    tot = a1 + a2
    e1 = sel * EXPERTS_PER_GROUP + i1
    e2 = sel * EXPERTS_PER_GROUP + i2
    return e1, e2, a1 / tot, a2 / tot


def _out_kernel(a_ref, b_ref, x_ref, wo_ref, g_ref, beta_ref, rwh_ref, rwl_ref, rb_ref, x1_ref, route_ref):
    half = MLA_HEADS * MLA_V
    mix = jnp.dot(a_ref[0], wo_ref[0:half, :], preferred_element_type=F32)
    mix = mix + jnp.dot(b_ref[0], wo_ref[half:2 * half, :], preferred_element_type=F32)
    x1 = _layer_norm(ALPHA * x_ref[0] + mix, g_ref[...], beta_ref[...])
    x1_ref[0] = x1
    xh = x1.astype(BF16)
    xl = (x1 - xh.astype(F32)).astype(BF16)
    rwh, rwl = rwh_ref[...], rwl_ref[...]
    logits = lax.dot_general(rwh, xh, NT_DIMS, preferred_element_type=F32)
    logits = logits + lax.dot_general(rwh, xl, NT_DIMS, preferred_element_type=F32)
    logits = logits + lax.dot_general(rwl, xh, NT_DIMS, preferred_element_type=F32)
    aff = 1.0 / (1.0 + jnp.exp(-logits))
    e1, e2, w1, w2 = _route(aff, aff + rb_ref[...])
    tm = logits.shape[1]
    rows = [e1.astype(F32), e2.astype(F32), w1, w2, jnp.zeros((LANE - 4, tm), F32)]
    route_ref[0] = jnp.concatenate(rows, axis=0).T


def _out_call(a, b, x, wo, g, beta, rwh, rwl, rb):
    B, T, _ = x.shape
    tm = TM_PROJ
    full = lambda shape: pl.BlockSpec(shape, lambda bb, t: (0,) * len(shape))
    tok = lambda d: pl.BlockSpec((1, tm, d), lambda bb, t: (bb, t, 0))
    return pl.pallas_call(
        _out_kernel,
        grid=(B, T // tm),
        in_specs=[tok(a.shape[-1]), tok(b.shape[-1]), tok(D_MODEL), full(wo.shape), full(g.shape), full(beta.shape),
                  full(rwh.shape), full(rwl.shape), full(rb.shape)],
        out_specs=[tok(D_MODEL), tok(LANE)],
        out_shape=[jax.ShapeDtypeStruct((B, T, D_MODEL), F32), jax.ShapeDtypeStruct((B, T, LANE), F32)],
        compiler_params=pltpu.CompilerParams(dimension_semantics=("parallel", "parallel")),
        name="outproj",
    )(a, b, x, wo, g, beta, rwh, rwl, rb)


def _route_positions(route, n_tiles):
    eid = route[:, 0:TOP_K].astype(I32).reshape(-1)
    onehot = (eid[:, None] == jnp.arange(N_EXPERTS, dtype=I32)[None, :]).astype(I32)
    csum = jnp.cumsum(onehot, axis=0)
    rank = jnp.sum(onehot * csum, axis=1) - 1
    counts = csum[-1]
    padded = ((counts + TR_MOE - 1) // TR_MOE) * TR_MOE
    ends = jnp.cumsum(padded)
    starts = ends - padded
    pos = jnp.sum(onehot * starts[None, :], axis=1) + rank
    n_active = ends[-1] // TR_MOE
    tile_row = jnp.minimum(jnp.arange(n_tiles, dtype=I32), n_active - 1) * TR_MOE
    tile_expert = jnp.sum((tile_row[:, None] >= ends[None, :]).astype(I32), axis=1)
    last_tile = jnp.where(padded > counts, ends // TR_MOE - 1, -1)
    spare = n_active + jnp.arange(N_EXPERTS, dtype=I32)
    fill = jnp.concatenate([last_tile, jnp.where(spare < n_tiles, spare, -1)])
    return pos.astype(I32), tile_expert.astype(I32), n_active.reshape(1).astype(I32), fill.astype(I32)


def _row_copy_loop(tm, copy):
    def body(q, carry):
        base = pl.multiple_of(q * SUBLANES, SUBLANES)
        for j in range(SUBLANES):
            for k in range(TOP_K):
                copy(base + j, (base + j) * TOP_K + k, k).start(priority=k)
        return carry

    lax.fori_loop(0, tm // SUBLANES, body, 0)


def _dispatch_kernel(pos_ref, fill_ref, x_ref, xs_ref, xr_ref, sem, zsem):
    tm = x_ref.shape[0]

    @pl.when(pl.program_id(0) == 0)
    def _():
        xr_ref[...] = jnp.zeros(xr_ref.shape, F32)

        def zero_copy(j):
            start = pl.multiple_of(jnp.maximum(fill_ref[j], 0) * tm, tm)
            return pltpu.make_async_copy(xr_ref, xs_ref.at[pl.ds(start, tm)], zsem)

        for j in range(fill_ref.shape[0]):
            @pl.when(fill_ref[j] >= 0)
            def _():
                zero_copy(j).start()

        for j in range(fill_ref.shape[0]):
            @pl.when(fill_ref[j] >= 0)
            def _():
                zero_copy(j).wait()

    xr_ref[...] = x_ref[...].reshape(tm, SUBLANES, LANE)
    _row_copy_loop(tm, lambda r, p, k: pltpu.make_async_copy(xr_ref.at[r], xs_ref.at[pos_ref[p]], sem))
    for k in range(TOP_K):
        pltpu.make_async_copy(xr_ref, xs_ref.at[pl.ds(0, tm)], sem).wait()


def _dispatch_call(pos, fill, x1, n_tiles):
    n = x1.shape[0]
    tm = TM_DISP
    assert tm == TR_MOE
    return pl.pallas_call(
        _dispatch_kernel,
        grid=(n // tm,),
        in_specs=[
            pl.BlockSpec((TOP_K * tm,), lambda i: (i,), memory_space=pltpu.SMEM),
            pl.BlockSpec(fill.shape, lambda i: (0,), memory_space=pltpu.SMEM),
            pl.BlockSpec((tm, D_MODEL), lambda i: (i, 0)),
        ],
        out_specs=pl.BlockSpec(memory_space=pl.ANY),
        out_shape=jax.ShapeDtypeStruct((n_tiles * TR_MOE, SUBLANES, LANE), F32),
        scratch_shapes=[pltpu.VMEM((tm, SUBLANES, LANE), F32), pltpu.SemaphoreType.DMA(()),
                        pltpu.SemaphoreType.DMA(())],
        compiler_params=pltpu.CompilerParams(dimension_semantics=("arbitrary",)),
        name="dispatch",
    )(pos, fill, x1)


def _expert_kernel(te_ref, na_ref, xs_ref, wg_ref, wu_ref, wd_ref, ys_ref):
    del te_ref
    active = pl.program_id(0) < na_ref[0]

    @pl.when(active)
    def _():
        tr = xs_ref.shape[0]
        xb = xs_ref[...].reshape(tr, D_MODEL).astype(BF16)
        hg = jnp.dot(xb, wg_ref[0, 0].astype(BF16), preferred_element_type=F32)
        hu = jnp.dot(xb, wu_ref[0, 0].astype(BF16), preferred_element_type=F32)
        act = (hg / (1.0 + jnp.exp(-hg))) * hu
        y = jnp.dot(act.astype(BF16), wd_ref[0, 0].astype(BF16), preferred_element_type=F32)
        ys_ref[...] = y.reshape(tr, SUBLANES, LANE)

    @pl.when(jnp.logical_not(active))
    def _():
        ys_ref[...] = jnp.zeros_like(ys_ref)


def _expert_call(tile_expert, n_active, xs, wg, wu, wd, layer):
    rows = xs.shape[0]
    tr = TR_MOE
    row_map = lambda i, te, na: (jnp.minimum(i, na[0] - 1), 0, 0)
    out_map = lambda i, te, na: (i, 0, 0)
    w_map = lambda i, te, na: (layer, te[i], 0, 0)
    return pl.pallas_call(
        _expert_kernel,
        grid_spec=pltpu.PrefetchScalarGridSpec(
            num_scalar_prefetch=2,
            grid=(rows // tr,),
            in_specs=[
                pl.BlockSpec((tr, SUBLANES, LANE), row_map),
                pl.BlockSpec((1, 1, D_MODEL, D_FF), w_map),
                pl.BlockSpec((1, 1, D_MODEL, D_FF), w_map),
                pl.BlockSpec((1, 1, D_FF, D_MODEL), w_map),
            ],
            out_specs=pl.BlockSpec((tr, SUBLANES, LANE), out_map),
        ),
        out_shape=jax.ShapeDtypeStruct(xs.shape, F32),
        compiler_params=pltpu.CompilerParams(dimension_semantics=("arbitrary",)),
        name="experts",
    )(tile_expert, n_active, xs, wg, wu, wd)


def _combine_kernel(pos_ref, posn_ref, x1_ref, route_ref, ys_ref, g_ref, beta_ref, o_ref, buf_a, buf_b, sem_a, sem_b):
    i = pl.program_id(0)
    n_steps = pl.num_programs(0)
    tm = x1_ref.shape[0]

    def row_copy(idx_ref, p, buf, k, r, sem):
        return pltpu.make_async_copy(ys_ref.at[idx_ref[p]], buf.at[k, r], sem)

    def wait_tile(buf, sem):
        for k in range(TOP_K):
            pltpu.make_async_copy(ys_ref.at[pl.ds(0, tm)], buf.at[k], sem).wait()

    @pl.when(i == 0)
    def _():
        _row_copy_loop(tm, lambda r, p, k: row_copy(pos_ref, p, buf_a, k, r, sem_a))

    def step(cur, sem_cur, nxt, sem_nxt):
        wait_tile(cur, sem_cur)
        for r in range(tm):
            for k in range(TOP_K):
                row_copy(posn_ref, TOP_K * r + k, nxt, k, r, sem_nxt).start(priority=k)
        rt = route_ref[...]
        ffn = rt[:, TOP_K:TOP_K + 1] * cur[0].reshape(tm, D_MODEL)
        for k in range(1, TOP_K):
            ffn = ffn + rt[:, TOP_K + k:TOP_K + k + 1] * cur[k].reshape(tm, D_MODEL)
        o_ref[...] = _layer_norm(ALPHA * x1_ref[...] + ffn, g_ref[...], beta_ref[...])

        @pl.when(i == n_steps - 1)
        def _():
            wait_tile(nxt, sem_nxt)

    @pl.when((i & 1) == 0)
    def _():
        step(buf_a, sem_a, buf_b, sem_b)

    @pl.when((i & 1) == 1)
    def _():
        step(buf_b, sem_b, buf_a, sem_a)


def _combine_call(pos, x1, route, ys, g, beta):
    n = x1.shape[0]
    tm = TM_DISP
    n_steps = n // tm
    vec = pl.BlockSpec((1, D_MODEL), lambda i: (0, 0))
    buf = pltpu.VMEM((TOP_K, tm, SUBLANES, LANE), F32)
    return pl.pallas_call(
        _combine_kernel,
        grid=(n_steps,),
        in_specs=[
            pl.BlockSpec((TOP_K * tm,), lambda i: (i,), memory_space=pltpu.SMEM),
            pl.BlockSpec((TOP_K * tm,), lambda i: (jnp.minimum(i + 1, n_steps - 1),), memory_space=pltpu.SMEM),
            pl.BlockSpec((tm, D_MODEL), lambda i: (i, 0)),
            pl.BlockSpec((tm, LANE), lambda i: (i, 0)),
            pl.BlockSpec(memory_space=pl.ANY),
            vec, vec,
        ],
        out_specs=pl.BlockSpec((tm, D_MODEL), lambda i: (i, 0)),
        out_shape=jax.ShapeDtypeStruct((n, D_MODEL), F32),
        scratch_shapes=[buf, buf, pltpu.SemaphoreType.DMA(()), pltpu.SemaphoreType.DMA(())],
        compiler_params=pltpu.CompilerParams(dimension_semantics=("arbitrary",)),
        name="combine",
    )(pos, pos, x1, route, ys, g, beta)


def _moe_call(x1, route, wg, wu, wd, layer, g, beta):
    n = x1.shape[0]
    n_tiles = TOP_K * n // TR_MOE + N_EXPERTS
    pos, tile_expert, n_active, fill = _route_positions(route, n_tiles)
    xs = _dispatch_call(pos, fill, x1, n_tiles)
    ys = _expert_call(tile_expert, n_active, xs, wg, wu, wd, layer)
    return _combine_call(pos, x1, route, ys, g, beta)


def _rope_tables(T):
    half = MLA_ROPE // 2
    pos = jnp.arange(T, dtype=F32)
    inv = ROPE_BASE ** (-jnp.arange(half, dtype=F32) / half)
    ang = pos[:, None] * inv[None, :]
    cos, sin = jnp.cos(ang), jnp.sin(ang)
    one = jnp.ones((T, MLA_NOPE), F32)
    z16 = jnp.zeros((T, half), F32)
    z32 = jnp.zeros((T, LANE - MLA_NOPE - MLA_ROPE), F32)
    z64 = jnp.zeros((T, MLA_NOPE), F32)
    rc = jnp.concatenate([one, cos, cos, z32], axis=1)
    rs1 = jnp.concatenate([z64, z16, sin, z32], axis=1)
    rs2 = jnp.concatenate([z64, -sin, z16, z32], axis=1)
    return rc, rs1, rs2


def _alibi_features(T):
    c = jnp.asarray(SLOPES, F32) * LOG2E
    c1 = c.astype(BF16).astype(F32)
    c2 = (c - c1).astype(BF16).astype(F32)
    c3 = (c - c1 - c2).astype(BF16).astype(F32)
    width = LANE - DSA_DIM
    qfeat = jnp.stack([c1, c1, c2, c2, c3, c3], axis=1)
    qfeat = jnp.pad(qfeat, ((0, 0), (0, width - qfeat.shape[1])))
    pos = jnp.arange(T, dtype=I32)
    hi = ((pos // 256) * 256).astype(F32)
    lo = (pos % 256).astype(F32)
    kfeat = jnp.stack([hi, lo, hi, lo, hi, lo], axis=1)
    kfeat = jnp.pad(kfeat, ((0, 0), (0, width - kfeat.shape[1])))
    return qfeat, kfeat


def _pack_w_in(w):
    sizes = (Q_LORA, KV_LORA, MLA_ROPE, DSA_HEADS * DSA_DIM, DSA_DIM, DSA_DIM, IDX_HEADS * IDX_DIM, IDX_DIM, IDX_HEADS)
    offs = np.concatenate([[0], np.cumsum(sizes)])
    w = w.astype(BF16)
    qa, kva, kr, dq, dk, dv, iq, ik, iw = [w[:, int(offs[j]):int(offs[j + 1])] for j in range(len(sizes))]
    z = lambda n: jnp.zeros((w.shape[0], n), w.dtype)
    cat = jnp.concatenate([
        qa, kva, dq, iq,
        z(MLA_NOPE), kr, z(LANE - MLA_NOPE - MLA_ROPE),
        dk, ik, z(LANE - DSA_DIM - IDX_DIM),
    ], axis=1)
    assert cat.shape[1] == C_END
    side = jnp.concatenate([dv, iw, z(R_END - DSA_DIM - IDX_HEADS)], axis=1).T
    return cat, side


def kernel(x, w_in, q_norm_g, w_q_up, kv_norm_g, w_uk, w_uv, w_o, ln1_g, ln1_b, router_w, router_bias,
           w_gate, w_up, w_down, ln2_g, ln2_b):
    B, T, D = x.shape
    rc, rs1, rs2 = _rope_tables(T)
    rwt = router_w.T
    rwh = rwt.astype(BF16)
    rwl = (rwt - rwh.astype(F32)).astype(BF16)
    rb = router_bias.reshape(N_EXPERTS, 1).astype(F32)
    qfeat, kfeat = _alibi_features(T)
    for l in range(DEPTH):
        wcat, wside = _pack_w_in(w_in[l])
        wq = w_q_up[l].reshape(Q_LORA, MLA_HEADS, MLA_NOPE + MLA_ROPE).transpose(1, 0, 2)
        wq = jnp.pad(wq, ((0, 0), (0, 0), (0, HEAD_PAD - MLA_NOPE - MLA_ROPE))).astype(BF16)
        wk = w_uk[l].reshape(KV_LORA, MLA_HEADS, MLA_NOPE).transpose(1, 0, 2)
        wk = jnp.pad(wk, ((0, 0), (0, 0), (0, HEAD_PAD - MLA_NOPE))).astype(BF16)
        wvt = w_uv[l].reshape(KV_LORA, MLA_HEADS, MLA_V).transpose(1, 2, 0).astype(BF16)
        q, k, vt, dq, dk, dvt, iq, ik, iwt = _proj_call(
            x, wcat, wside, q_norm_g[l].reshape(1, Q_LORA), kv_norm_g[l].reshape(1, KV_LORA), wq, wk, wvt,
            rc, rs1, rs2, qfeat, kfeat)
        out_a = _mla_call(q, k, vt)
        out_b = _dsa_call(dq, iq, iwt, dk, dvt, ik)
        x1, route = _out_call(out_a, out_b, x, w_o[l].astype(BF16), ln1_g[l].reshape(1, D), ln1_b[l].reshape(1, D),
                              rwh, rwl, rb)
        y = _moe_call(x1.reshape(B * T, D), route.reshape(B * T, LANE), w_gate, w_up,
                      w_down, l, ln2_g[l].reshape(1, D), ln2_b[l].reshape(1, D))
        x = y.reshape(B, T, D)
    return x
```

```python
import jax
import jax.numpy as jnp
import numpy as np
from jax import lax
from jax.experimental import pallas as pl
from jax.experimental.pallas import tpu as pltpu

F32 = jnp.float32
BF16 = jnp.bfloat16
I32 = jnp.int32

D_MODEL = 1024
DEPTH = 2
MLA_HEADS = 8
MLA_NOPE = 64
MLA_ROPE = 32
MLA_V = 64
Q_LORA = 384
KV_LORA = 256
ROPE_BASE = 10000.0
DSA_HEADS = 8
DSA_DIM = 64
IDX_HEADS = 8
IDX_DIM = 32
TOPK_MAX = 256
N_EXPERTS = 16
N_GROUPS = 4
EXPERTS_PER_GROUP = 4
D_FF = 512
ALPHA = (2.0 * DEPTH) ** 0.25
LN_EPS = 1e-5
RMS_EPS = 1e-6
MLA_SCALE = (MLA_NOPE + MLA_ROPE) ** -0.5
LOG2E = 1.4426950408889634
SLOPES = tuple(2.0 ** (-8.0 * (h + 1) / DSA_HEADS) for h in range(DSA_HEADS))

V_AUG = 80
LANE = 128
SUBLANES = 8
HEAD_PAD = LANE
NEG = -1e30

C_QA = 0
C_KVA = C_QA + Q_LORA
C_DQ = C_KVA + KV_LORA
C_IQ = C_DQ + DSA_HEADS * DSA_DIM
C_KR = C_IQ + IDX_HEADS * IDX_DIM
C_KX = C_KR + LANE
C_END = C_KX + LANE
R_DV = 0
R_IW = DSA_DIM
R_END = LANE
IDX_SCALE = (IDX_DIM * IDX_HEADS) ** -0.5
DSA_SCALE = DSA_DIM ** -0.5

TM_PROJ = 512
TM_OUT = 1024
TM_OUT_SUB = 512
TQ_MLA = 256
TQ_DSA = 256
TK_DSA = 256
TOP_K = 2
TR_MOE = 512
TM_DISP = 512

NT_DIMS = (((1,), (1,)), ((), ()))


def _rms(x, g):
    return x * lax.rsqrt(jnp.mean(x * x, axis=-1, keepdims=True) + RMS_EPS) * g


def _layer_norm(y, g, b):
    mu = jnp.mean(y, axis=-1, keepdims=True)
    d = y - mu
    var = jnp.mean(d * d, axis=-1, keepdims=True)
    return d * lax.rsqrt(var + LN_EPS) * g + b


def _proj_kernel(x_ref, wcat_ref, wside_ref, qg_ref, kvg_ref, wq_ref, wk_ref, wv_ref, rc_ref, rs1_ref, rs2_ref,
                 qfeat_ref, kfeat_ref,
                 q_ref, k_ref, vt_ref, dq_ref, dk_ref, dvt_ref, iq_ref, ik_ref, iwt_ref):
    xb = x_ref[0].astype(BF16)
    proj = jnp.dot(xb, wcat_ref[...], preferred_element_type=F32)
    side = lax.dot_general(wside_ref[...], xb, NT_DIMS, preferred_element_type=F32)
    qn = _rms(proj[:, C_QA:C_QA + Q_LORA], qg_ref[...]).astype(BF16)
    ckv = _rms(proj[:, C_KVA:C_KVA + KV_LORA], kvg_ref[...]).astype(BF16)
    rc, rs1, rs2 = rc_ref[...], rs1_ref[...], rs2_ref[...]

    def rope(t):
        return t * rc + pltpu.roll(t, 16, 1) * rs1 + pltpu.roll(t, LANE - 16, 1) * rs2

    kr = rope(proj[:, C_KR:C_KR + LANE])
    tm = xb.shape[0]
    aug = jnp.concatenate([jnp.ones((1, tm), F32), jnp.zeros((V_AUG - MLA_V - 1, tm), F32)], axis=0)
    vt_all = lax.dot_general(wv_ref[...], ckv, NT_DIMS, preferred_element_type=F32)
    for h in range(MLA_HEADS):
        if h % 2 == 0:
            q2h = jnp.dot(qn, wq_ref[h // 2], preferred_element_type=F32)
            k2h = jnp.dot(ckv, wk_ref[h // 2], preferred_element_type=F32)
        half = slice(HEAD_PAD * (h % 2), HEAD_PAD * (h % 2 + 1))
        q_ref[0, h] = (rope(q2h[:, half]) * (MLA_SCALE * LOG2E)).astype(BF16)
        k_ref[0, h] = (k2h[:, half] + kr).astype(BF16)
        vth = jnp.concatenate([vt_all[MLA_V * h:MLA_V * (h + 1)], aug], axis=0)
        for j in range(tm // TQ_MLA):
            vt_ref[0, h, j] = vth[:, TQ_MLA * j:TQ_MLA * (j + 1)].astype(BF16)
    for h in range(DSA_HEADS):
        dqh = proj[:, C_DQ + DSA_DIM * h:C_DQ + DSA_DIM * (h + 1)] * (DSA_SCALE * LOG2E)
        feat = jnp.broadcast_to(qfeat_ref[h:h + 1, :], (tm, LANE - DSA_DIM))
        dq_ref[0, h] = jnp.concatenate([dqh, feat], axis=1).astype(BF16)
    for h in range(IDX_HEADS):
        iq_ref[0, h] = proj[:, C_IQ + IDX_DIM * h:C_IQ + IDX_DIM * (h + 1)].astype(BF16)
    dk_ref[0] = jnp.concatenate([proj[:, C_KX:C_KX + DSA_DIM], kfeat_ref[...]], axis=1).astype(BF16)
    ik_ref[0] = proj[:, C_KX + DSA_DIM:C_KX + DSA_DIM + IDX_DIM].astype(BF16)
    dvt = jnp.concatenate([side[R_DV:R_DV + DSA_DIM], aug], axis=0)
    for j in range(TM_PROJ // TK_DSA):
        dvt_ref[0, j] = dvt[:, TK_DSA * j:TK_DSA * (j + 1)].astype(BF16)
    iwt_ref[0] = side[R_IW:R_IW + IDX_HEADS, :] * IDX_SCALE


def _proj_call(x, wcat, wside, qg, kvg, wq, wk, wv, rc, rs1, rs2, qfeat, kfeat):
    B, T, _ = x.shape
    tm = TM_PROJ
    full = lambda shape: pl.BlockSpec(shape, lambda b, t: (0,) * len(shape))
    head_out = lambda d: pl.BlockSpec((1, MLA_HEADS, tm, d), lambda b, t: (b, 0, t, 0))
    tok_out = lambda d: pl.BlockSpec((1, tm, d), lambda b, t: (b, t, 0))
    tab = pl.BlockSpec((tm, LANE), lambda b, t: (t, 0))
    return pl.pallas_call(
        _proj_kernel,
        grid=(B, T // tm),
        in_specs=[
            pl.BlockSpec((1, tm, D_MODEL), lambda b, t: (b, t, 0)),
            full(wcat.shape), full(wside.shape), full(qg.shape), full(kvg.shape), full(wq.shape), full(wk.shape),
            full(wv.shape), tab, tab, tab, full(qfeat.shape),
            pl.BlockSpec((tm, LANE - DSA_DIM), lambda b, t: (t, 0)),
        ],
        out_specs=[head_out(HEAD_PAD), head_out(HEAD_PAD),
                   pl.BlockSpec((1, MLA_HEADS, tm // TQ_MLA, V_AUG, TQ_MLA), lambda b, t: (b, 0, t, 0, 0)),
                   head_out(LANE),
                   tok_out(LANE),
                   pl.BlockSpec((1, tm // TK_DSA, V_AUG, TK_DSA), lambda b, t: (b, t, 0, 0)),
                   head_out(IDX_DIM), tok_out(IDX_DIM),
                   pl.BlockSpec((1, IDX_HEADS, tm), lambda b, t: (b, 0, t))],
        out_shape=[
            jax.ShapeDtypeStruct((B, MLA_HEADS, T, HEAD_PAD), BF16),
            jax.ShapeDtypeStruct((B, MLA_HEADS, T, HEAD_PAD), BF16),
            jax.ShapeDtypeStruct((B, MLA_HEADS, T // TQ_MLA, V_AUG, TQ_MLA), BF16),
            jax.ShapeDtypeStruct((B, DSA_HEADS, T, LANE), BF16),
            jax.ShapeDtypeStruct((B, T, LANE), BF16),
            jax.ShapeDtypeStruct((B, T // TK_DSA, V_AUG, TK_DSA), BF16),
            jax.ShapeDtypeStruct((B, IDX_HEADS, T, IDX_DIM), BF16),
            jax.ShapeDtypeStruct((B, T, IDX_DIM), BF16),
            jax.ShapeDtypeStruct((B, IDX_HEADS, T), F32),
        ],
        compiler_params=pltpu.CompilerParams(dimension_semantics=("parallel", "parallel")),
        name="proj",
    )(x, wcat, wside, qg, kvg, wq, wk, wv, rc, rs1, rs2, qfeat, kfeat)


def _transposed_heads_store(o_ref, ot, heads, width, tq):
    zpad = jnp.zeros((LANE - width, tq), F32)
    for h in range(heads):
        oh = jnp.concatenate([ot[:, tq * h:tq * (h + 1)], zpad], axis=0).T
        o_ref[0, :, width * h:width * (h + 1)] = oh[:, 0:width].astype(BF16)


def _mla_kernel(q_ref, k_ref, vt_ref, o_ref, m_ref, acc_ref, base_ref):
    i = pl.program_id(1)
    tq = TQ_MLA
    heads = MLA_HEADS
    m_ref[...] = jnp.full(m_ref.shape, NEG, F32)
    acc_ref[...] = jnp.zeros(acc_ref.shape, F32)
    key = lax.broadcasted_iota(I32, (tq, tq), 0)
    qpos = lax.broadcasted_iota(I32, (tq, tq), 1)
    base_ref[...] = jnp.where(key <= qpos, -NEG, NEG)

    def step(j, masked):
        start = pl.multiple_of(j * tq, tq)
        s_list = [lax.dot_general(k_ref[0, h, pl.ds(start, tq), :], q_ref[0, h], NT_DIMS,
                                  preferred_element_type=F32) for h in range(heads)]
        m_all, acc_all = m_ref[...], acc_ref[...]
        m_out, a_out, pv_out = [], [], []
        for h in range(heads):
            lanes = slice(tq * h, tq * (h + 1))
            s = jnp.minimum(s_list[h], base_ref[...]) if masked else s_list[h]
            m_prev = m_all[:, lanes]
            m_new = jnp.maximum(m_prev, jnp.max(s, axis=0, keepdims=True))
            p = jnp.exp2(s - m_new)
            pv_out.append(jnp.dot(vt_ref[0, h, j], p.astype(BF16), preferred_element_type=F32))
            a_out.append(jnp.exp2(m_prev - m_new))
            m_out.append(m_new)
        m_ref[...] = jnp.concatenate(m_out, axis=1)
        acc_ref[...] = jnp.concatenate(a_out, axis=1) * acc_all + jnp.concatenate(pv_out, axis=1)

    def body_pair(j, carry):
        step(2 * j, False)
        step(2 * j + 1, False)
        return carry

    def body_odd_tail(j, carry):
        step(i - 1, False)
        step(i, True)
        return carry

    def body_even_tail(j, carry):
        step(i, True)
        return carry

    odd = i & 1
    lax.fori_loop(0, lax.shift_right_logical(i, 1), body_pair, 0)
    lax.fori_loop(0, odd, body_odd_tail, 0)
    lax.fori_loop(0, 1 - odd, body_even_tail, 0)
    _transposed_heads_store(o_ref, acc_ref[0:MLA_V, :] / acc_ref[MLA_V:MLA_V + 1, :], heads, MLA_V, tq)


def _mla_call(q, k, vt):
    B, H, T, _ = q.shape
    tq = TQ_MLA
    return pl.pallas_call(
        _mla_kernel,
        grid=(B, T // tq),
        in_specs=[
            pl.BlockSpec((1, H, tq, HEAD_PAD), lambda b, i: (b, 0, i, 0)),
            pl.BlockSpec((1, H, T, HEAD_PAD), lambda b, i: (b, 0, 0, 0)),
            pl.BlockSpec((1, H, T // tq, V_AUG, tq), lambda b, i: (b, 0, 0, 0, 0)),
        ],
        out_specs=pl.BlockSpec((1, tq, H * MLA_V), lambda b, i: (b, i, 0)),
        out_shape=jax.ShapeDtypeStruct((B, T, H * MLA_V), BF16),
        scratch_shapes=[
            pltpu.VMEM((1, H * tq), F32),
            pltpu.VMEM((V_AUG, H * tq), F32),
            pltpu.VMEM((tq, tq), F32),
        ],
        compiler_params=pltpu.CompilerParams(dimension_semantics=("parallel", "arbitrary")),
        name="mla",
    )(q, k, vt)


def _dsa_kernel(dq_ref, iq_ref, iwt_ref, dk_ref, dvt_ref, ik_ref, o_ref, sc_ref, m_ref, acc_ref):
    i = pl.program_id(1)
    tq, tk = TQ_DSA, TK_DSA
    T = dk_ref.shape[1]
    nch = lax.div((i + 1) * tq + (tk - 1), tk)
    cols = DSA_HEADS * tq
    key = lax.broadcasted_iota(I32, (tk, tq), 0)
    t_q = i * tq + lax.broadcasted_iota(I32, (tk, tq), 1)
    t_row = i * tq + lax.broadcasted_iota(I32, (1, tq), 1)

    iq2 = iq_ref[0].reshape(cols, IDX_DIM)
    w8 = iwt_ref[0]

    def for_chunk_pairs(step):
        def pair_body(j, carry):
            step(2 * j)
            step(2 * j + 1)
            return carry

        lax.fori_loop(0, lax.shift_right_logical(nch, 1), pair_body, 0)

        @pl.when((nch & 1) == 1)
        def _():
            step(nch - 1)

    def score_step(c):
        start = pl.multiple_of(c * tk, tk)
        ikc = ik_ref[0, pl.ds(start, tk), :]
        rel = lax.dot_general(ikc, iq2, NT_DIMS, preferred_element_type=F32)
        sc = jnp.maximum(rel[:, 0:tq], 0.0) * w8[0:1]
        for h in range(1, IDX_HEADS):
            sc = sc + jnp.maximum(rel[:, tq * h:tq * (h + 1)], 0.0) * w8[h:h + 1]
        sc_ref[c] = jnp.where(key + c * tk <= t_q, sc, -jnp.inf)

    for_chunk_pairs(score_step)

    def count(pred):
        def hits(c):
            hit = jnp.where(pred(sc_ref[c], key + c * tk), 1.0, 0.0)
            return jnp.sum(hit.reshape(tk // SUBLANES, SUBLANES, tq), axis=0)

        def pair_body(j, acc):
            return acc + hits(2 * j) + hits(2 * j + 1)

        acc = lax.fori_loop(0, lax.shift_right_logical(nch, 1), pair_body, jnp.zeros((SUBLANES, tq), F32))
        acc = acc + lax.cond((nch & 1) == 1, lambda: hits(nch - 1), lambda: jnp.zeros((SUBLANES, tq), F32))
        return jnp.sum(acc, axis=0, keepdims=True)

    int_min = jnp.int32(-2 ** 31)

    def key_to_f32(ku):
        ks = ku ^ int_min
        bits = jnp.where(ks >= 0, ks, ks ^ jnp.int32(0x7FFFFFFF))
        return lax.bitcast_convert_type(bits, F32)

    kf = float(TOPK_MAX)

    def bis_body(step, carry):
        res, cnt_res = carry
        cand = res | lax.shift_left(jnp.int32(1), 31 - step)
        tau_c = key_to_f32(cand)
        cnt = count(lambda sc, idx: sc >= tau_c)
        ok = cnt >= kf
        return jnp.where(ok, cand, res), jnp.where(ok, cnt, cnt_res)

    bis_init = (jnp.zeros((1, tq), I32), jnp.full((1, tq), kf, F32))
    short = t_row < TOPK_MAX
    res, cnt_ge = lax.cond((i + 1) * tq > TOPK_MAX, lambda: lax.fori_loop(0, 32, bis_body, bis_init),
                           lambda: bis_init)
    tau = jnp.where(short, -jnp.inf, key_to_f32(res))
    cnt_ge = jnp.where(short, kf, cnt_ge)

    def tie_search():
        need = float(TOPK_MAX) - count(lambda sc, idx: sc > tau)

        def tie_body(step, jm):
            cand = jm | lax.shift_left(jnp.int32(1), 10 - step)
            cnt = count(lambda sc, idx: (sc >= tau) & jnp.logical_not(sc > tau) & (idx < cand))
            return jnp.where(cnt < need, cand, jm)

        return lax.fori_loop(0, 11, tie_body, jnp.zeros((1, tq), I32))

    j_max = lax.cond(jnp.max(cnt_ge) > float(TOPK_MAX), tie_search, lambda: jnp.full((1, tq), T, I32))

    q2 = dq_ref[0].reshape(cols, LANE)
    m_ref[...] = jnp.full((1, cols), NEG, F32)
    acc_ref[...] = jnp.zeros((V_AUG, cols), F32)

    def att_step(c):
        start = pl.multiple_of(c * tk, tk)
        kc = dk_ref[0, pl.ds(start, tk), :]
        vt = dvt_ref[0, c]
        sc = sc_ref[c]
        idx = key + c * tk
        sel = (sc >= tau) & ((sc > tau) | (idx <= j_max)) & (sc > -jnp.inf)
        cap = jnp.where(sel, -NEG, NEG)
        m_all, acc_all = m_ref[...], acc_ref[...]
        m_out, a_out, p_out = [], [], []
        s_all = lax.dot_general(kc, q2, NT_DIMS, preferred_element_type=F32)
        for h in range(DSA_HEADS):
            lanes = slice(tq * h, tq * (h + 1))
            s = jnp.minimum(s_all[:, lanes], cap)
            m_prev = m_all[:, lanes]
            m_new = jnp.maximum(m_prev, jnp.max(s, axis=0, keepdims=True))
            p_out.append(jnp.exp2(s - m_new).astype(BF16))
            a_out.append(jnp.exp2(m_prev - m_new))
            m_out.append(m_new)
        pv = jnp.dot(vt, jnp.concatenate(p_out, axis=1), preferred_element_type=F32)
        m_ref[...] = jnp.concatenate(m_out, axis=1)
        acc_ref[...] = jnp.concatenate(a_out, axis=1) * acc_all + pv

    for_chunk_pairs(att_step)
    _transposed_heads_store(o_ref, acc_ref[0:DSA_DIM, :] / acc_ref[DSA_DIM:DSA_DIM + 1, :], DSA_HEADS, DSA_DIM, tq)


def _dsa_call(dq, iq, iwt, dk, dvt, ik):
    B, H, T, _ = dq.shape
    tq, tk = TQ_DSA, TK_DSA
    cols = H * tq
    return pl.pallas_call(
        _dsa_kernel,
        grid=(B, T // tq),
        in_specs=[
            pl.BlockSpec((1, H, tq, LANE), lambda b, i: (b, 0, i, 0)),
            pl.BlockSpec((1, H, tq, IDX_DIM), lambda b, i: (b, 0, i, 0)),
            pl.BlockSpec((1, IDX_HEADS, tq), lambda b, i: (b, 0, i)),
            pl.BlockSpec((1, T, LANE), lambda b, i: (b, 0, 0)),
            pl.BlockSpec((1, T // tk, V_AUG, tk), lambda b, i: (b, 0, 0, 0)),
            pl.BlockSpec((1, T, IDX_DIM), lambda b, i: (b, 0, 0)),
        ],
        out_specs=pl.BlockSpec((1, tq, H * DSA_DIM), lambda b, i: (b, i, 0)),
        out_shape=jax.ShapeDtypeStruct((B, T, H * DSA_DIM), BF16),
        scratch_shapes=[
            pltpu.VMEM((T // tk, tk, tq), F32),
            pltpu.VMEM((1, cols), F32),
            pltpu.VMEM((V_AUG, cols), F32),
        ],
        compiler_params=pltpu.CompilerParams(dimension_semantics=("parallel", "arbitrary")),
        name="dsa",
    )(dq, iq, iwt, dk, dvt, ik)


def _route(aff, biased):
    b = [biased[e:e + 1] for e in range(N_EXPERTS)]
    a = [aff[e:e + 1] for e in range(N_EXPERTS)]
    gs = []
    for g in range(N_GROUPS):
        v0, v1, v2, v3 = b[4 * g:4 * g + 4]
        hi1, lo1 = jnp.maximum(v0, v1), jnp.minimum(v0, v1)
        hi2, lo2 = jnp.maximum(v2, v3), jnp.minimum(v2, v3)
        gs.append(jnp.maximum(hi1, hi2) + jnp.maximum(jnp.minimum(hi1, hi2), jnp.maximum(lo1, lo2)))
    best, sel = gs[0], jnp.zeros_like(gs[0], dtype=I32)
    for g in range(1, N_GROUPS):
        upd = gs[g] > best
        sel = jnp.where(upd, g, sel)
        best = jnp.where(upd, gs[g], best)
    v, av = [], []
    for k in range(EXPERTS_PER_GROUP):
        vk, ak = b[k], a[k]
        for g in range(1, N_GROUPS):
            vk = jnp.where(sel == g, b[4 * g + k], vk)
            ak = jnp.where(sel == g, a[4 * g + k], ak)
        v.append(vk)
        av.append(ak)
    m1, i1 = v[0], jnp.zeros_like(sel)
    for k in range(1, EXPERTS_PER_GROUP):
        upd = v[k] > m1
        i1 = jnp.where(upd, k, i1)
        m1 = jnp.where(upd, v[k], m1)
    m2, i2 = jnp.full_like(m1, -jnp.inf), jnp.zeros_like(sel)
    for k in range(EXPERTS_PER_GROUP):
        cand = jnp.where(i1 == k, -jnp.inf, v[k])
        upd = cand > m2
        i2 = jnp.where(upd, k, i2)
        m2 = jnp.where(upd, cand, m2)
    a1, a2 = jnp.zeros_like(m1), jnp.zeros_like(m1)
    for k in range(EXPERTS_PER_GROUP):
        a1 = jnp.where(i1 == k, av[k], a1)
        a2 = jnp.where(i2 == k, av[k], a2)
    tot = a1 + a2
    e1 = sel * EXPERTS_PER_GROUP + i1
    e2 = sel * EXPERTS_PER_GROUP + i2
    return e1, e2, a1 / tot, a2 / tot


def _out_kernel(a_ref, b_ref, x_ref, wo_ref, g_ref, beta_ref, rwh_ref, rwl_ref, rb_ref, x1_ref, route_ref):
    half = MLA_HEADS * MLA_V
    rwh, rwl = rwh_ref[...], rwl_ref[...]
    for s in range(TM_OUT // TM_OUT_SUB):
        rows = slice(TM_OUT_SUB * s, TM_OUT_SUB * (s + 1))
        mix = jnp.dot(a_ref[0, rows, :], wo_ref[0:half, :], preferred_element_type=F32)
        mix = mix + jnp.dot(b_ref[0, rows, :], wo_ref[half:2 * half, :], preferred_element_type=F32)
        x1 = _layer_norm(ALPHA * x_ref[0, rows, :] + mix, g_ref[...], beta_ref[...])
        x1_ref[0, rows, :] = x1
        xh = x1.astype(BF16)
        xl = (x1 - xh.astype(F32)).astype(BF16)
        logits = lax.dot_general(rwh, xh, NT_DIMS, preferred_element_type=F32)
        logits = logits + lax.dot_general(rwh, xl, NT_DIMS, preferred_element_type=F32)
        logits = logits + lax.dot_general(rwl, xh, NT_DIMS, preferred_element_type=F32)
        aff = 1.0 / (1.0 + jnp.exp(-logits))
        e1, e2, w1, w2 = _route(aff, aff + rb_ref[...])
        info = [e1.astype(F32), e2.astype(F32), w1, w2, jnp.zeros((LANE - 4, TM_OUT_SUB), F32)]
        route_ref[0, rows, :] = jnp.concatenate(info, axis=0).T


def _out_call(a, b, x, wo, g, beta, rwh, rwl, rb):
    B, T, _ = x.shape
    tm = TM_OUT
    full = lambda shape: pl.BlockSpec(shape, lambda bb, t: (0,) * len(shape))
    tok = lambda d: pl.BlockSpec((1, tm, d), lambda bb, t: (bb, t, 0))
    return pl.pallas_call(
        _out_kernel,
        grid=(B, T // tm),
        in_specs=[tok(a.shape[-1]), tok(b.shape[-1]), tok(D_MODEL), full(wo.shape), full(g.shape), full(beta.shape),
                  full(rwh.shape), full(rwl.shape), full(rb.shape)],
        out_specs=[tok(D_MODEL), tok(LANE)],
        out_shape=[jax.ShapeDtypeStruct((B, T, D_MODEL), F32), jax.ShapeDtypeStruct((B, T, LANE), F32)],
        compiler_params=pltpu.CompilerParams(dimension_semantics=("parallel", "parallel")),
        name="outproj",
    )(a, b, x, wo, g, beta, rwh, rwl, rb)


def _route_positions(route, n_tiles):
    eid = route[:, 0:TOP_K].astype(I32).reshape(-1)
    onehot = (eid[:, None] == jnp.arange(N_EXPERTS, dtype=I32)[None, :]).astype(I32)
    csum = jnp.cumsum(onehot, axis=0)
    rank = jnp.sum(onehot * csum, axis=1) - 1
    counts = csum[-1]
    padded = ((counts + TR_MOE - 1) // TR_MOE) * TR_MOE
    ends = jnp.cumsum(padded)
    starts = ends - padded
    pos = jnp.sum(onehot * starts[None, :], axis=1) + rank
    n_active = ends[-1] // TR_MOE
    tile_row = jnp.minimum(jnp.arange(n_tiles, dtype=I32), n_active - 1) * TR_MOE
    tile_expert = jnp.sum((tile_row[:, None] >= ends[None, :]).astype(I32), axis=1)
    last_tile = jnp.where(padded > counts, ends // TR_MOE - 1, -1)
    spare = n_active + jnp.arange(N_EXPERTS, dtype=I32)
    fill = jnp.concatenate([last_tile, jnp.where(spare < n_tiles, spare, -1)])
    return pos.astype(I32), tile_expert.astype(I32), n_active.reshape(1).astype(I32), fill.astype(I32)


def _row_copy_loop(tm, copy):
    def body(q, carry):
        base = pl.multiple_of(q * SUBLANES, SUBLANES)
        for j in range(SUBLANES):
            for k in range(TOP_K):
                copy(base + j, (base + j) * TOP_K + k, k).start(priority=k)
        return carry

    lax.fori_loop(0, tm // SUBLANES, body, 0)


def _dispatch_kernel(pos_ref, fill_ref, x_ref, xs_ref, xr_ref, sem, zsem):
    tm = x_ref.shape[0]

    @pl.when(pl.program_id(0) == 0)
    def _():
        xr_ref[...] = jnp.zeros(xr_ref.shape, F32)

        def zero_copy(j):
            start = pl.multiple_of(jnp.maximum(fill_ref[j], 0) * tm, tm)
            return pltpu.make_async_copy(xr_ref, xs_ref.at[pl.ds(start, tm)], zsem)

        for j in range(fill_ref.shape[0]):
            @pl.when(fill_ref[j] >= 0)
            def _():
                zero_copy(j).start()

        for j in range(fill_ref.shape[0]):
            @pl.when(fill_ref[j] >= 0)
            def _():
                zero_copy(j).wait()

    xr_ref[...] = x_ref[...].reshape(tm, SUBLANES, LANE)
    _row_copy_loop(tm, lambda r, p, k: pltpu.make_async_copy(xr_ref.at[r], xs_ref.at[pos_ref[p]], sem))
    for k in range(TOP_K):
        pltpu.make_async_copy(xr_ref, xs_ref.at[pl.ds(0, tm)], sem).wait()


def _dispatch_call(pos, fill, x1, n_tiles):
    n = x1.shape[0]
    tm = TM_DISP
    assert tm == TR_MOE
    return pl.pallas_call(
        _dispatch_kernel,
        grid=(n // tm,),
        in_specs=[
            pl.BlockSpec((TOP_K * tm,), lambda i: (i,), memory_space=pltpu.SMEM),
            pl.BlockSpec(fill.shape, lambda i: (0,), memory_space=pltpu.SMEM),
            pl.BlockSpec((tm, D_MODEL), lambda i: (i, 0)),
        ],
        out_specs=pl.BlockSpec(memory_space=pl.ANY),
        out_shape=jax.ShapeDtypeStruct((n_tiles * TR_MOE, SUBLANES, LANE), F32),
        scratch_shapes=[pltpu.VMEM((tm, SUBLANES, LANE), F32), pltpu.SemaphoreType.DMA(()),
                        pltpu.SemaphoreType.DMA(())],
        compiler_params=pltpu.CompilerParams(dimension_semantics=("arbitrary",)),
        name="dispatch",
    )(pos, fill, x1)


def _expert_kernel(te_ref, na_ref, xs_ref, wg_ref, wu_ref, wd_ref, ys_ref):
    del te_ref
    active = pl.program_id(0) < na_ref[0]

    @pl.when(active)
    def _():
        tr = xs_ref.shape[0]
        xb = xs_ref[...].reshape(tr, D_MODEL).astype(BF16)
        hg = jnp.dot(xb, wg_ref[0, 0].astype(BF16), preferred_element_type=F32)
        hu = jnp.dot(xb, wu_ref[0, 0].astype(BF16), preferred_element_type=F32)
        act = (hg / (1.0 + jnp.exp(-hg))) * hu
        y = jnp.dot(act.astype(BF16), wd_ref[0, 0].astype(BF16), preferred_element_type=F32)
        ys_ref[...] = y.reshape(tr, SUBLANES, LANE)

    @pl.when(jnp.logical_not(active))
    def _():
        ys_ref[...] = jnp.zeros_like(ys_ref)


def _expert_call(tile_expert, n_active, xs, wg, wu, wd, layer):
    rows = xs.shape[0]
    tr = TR_MOE
    row_map = lambda i, te, na: (jnp.minimum(i, na[0] - 1), 0, 0)
    out_map = lambda i, te, na: (i, 0, 0)
    w_map = lambda i, te, na: (layer, te[i], 0, 0)
    return pl.pallas_call(
        _expert_kernel,
        grid_spec=pltpu.PrefetchScalarGridSpec(
            num_scalar_prefetch=2,
            grid=(rows // tr,),
            in_specs=[
                pl.BlockSpec((tr, SUBLANES, LANE), row_map),
                pl.BlockSpec((1, 1, D_MODEL, D_FF), w_map),
                pl.BlockSpec((1, 1, D_MODEL, D_FF), w_map),
                pl.BlockSpec((1, 1, D_FF, D_MODEL), w_map),
            ],
            out_specs=pl.BlockSpec((tr, SUBLANES, LANE), out_map),
        ),
        out_shape=jax.ShapeDtypeStruct(xs.shape, F32),
        compiler_params=pltpu.CompilerParams(dimension_semantics=("arbitrary",)),
        name="experts",
    )(tile_expert, n_active, xs, wg, wu, wd)


def _combine_kernel(pos_ref, posn_ref, x1_ref, route_ref, ys_ref, g_ref, beta_ref, o_ref, buf_a, buf_b, sem_a, sem_b):
    i = pl.program_id(0)
    n_steps = pl.num_programs(0)
    tm = x1_ref.shape[0]

    def row_copy(idx_ref, p, buf, k, r, sem):
        return pltpu.make_async_copy(ys_ref.at[idx_ref[p]], buf.at[k, r], sem)

    def wait_tile(buf, sem):
        for k in range(TOP_K):
            pltpu.make_async_copy(ys_ref.at[pl.ds(0, tm)], buf.at[k], sem).wait()

    @pl.when(i == 0)
    def _():
        _row_copy_loop(tm, lambda r, p, k: row_copy(pos_ref, p, buf_a, k, r, sem_a))

    def step(cur, sem_cur, nxt, sem_nxt):
        wait_tile(cur, sem_cur)
        for r in range(tm):
            for k in range(TOP_K):
                row_copy(posn_ref, TOP_K * r + k, nxt, k, r, sem_nxt).start(priority=k)
        rt = route_ref[...]
        ffn = rt[:, TOP_K:TOP_K + 1] * cur[0].reshape(tm, D_MODEL)
        for k in range(1, TOP_K):
            ffn = ffn + rt[:, TOP_K + k:TOP_K + k + 1] * cur[k].reshape(tm, D_MODEL)
        o_ref[...] = _layer_norm(ALPHA * x1_ref[...] + ffn, g_ref[...], beta_ref[...])

        @pl.when(i == n_steps - 1)
        def _():
            wait_tile(nxt, sem_nxt)

    @pl.when((i & 1) == 0)
    def _():
        step(buf_a, sem_a, buf_b, sem_b)

    @pl.when((i & 1) == 1)
    def _():
        step(buf_b, sem_b, buf_a, sem_a)


def _combine_call(pos, x1, route, ys, g, beta):
    n = x1.shape[0]
    tm = TM_DISP
    n_steps = n // tm
    vec = pl.BlockSpec((1, D_MODEL), lambda i: (0, 0))
    buf = pltpu.VMEM((TOP_K, tm, SUBLANES, LANE), F32)
    return pl.pallas_call(
        _combine_kernel,
        grid=(n_steps,),
        in_specs=[
            pl.BlockSpec((TOP_K * tm,), lambda i: (i,), memory_space=pltpu.SMEM),
            pl.BlockSpec((TOP_K * tm,), lambda i: (jnp.minimum(i + 1, n_steps - 1),), memory_space=pltpu.SMEM),
            pl.BlockSpec((tm, D_MODEL), lambda i: (i, 0)),
            pl.BlockSpec((tm, LANE), lambda i: (i, 0)),
            pl.BlockSpec(memory_space=pl.ANY),
            vec, vec,
        ],
        out_specs=pl.BlockSpec((tm, D_MODEL), lambda i: (i, 0)),
        out_shape=jax.ShapeDtypeStruct((n, D_MODEL), F32),
        scratch_shapes=[buf, buf, pltpu.SemaphoreType.DMA(()), pltpu.SemaphoreType.DMA(())],
        compiler_params=pltpu.CompilerParams(dimension_semantics=("arbitrary",)),
        name="combine",
    )(pos, pos, x1, route, ys, g, beta)


def _moe_call(x1, route, wg, wu, wd, layer, g, beta):
    n = x1.shape[0]
    n_tiles = TOP_K * n // TR_MOE + N_EXPERTS
    pos, tile_expert, n_active, fill = _route_positions(route, n_tiles)
    xs = _dispatch_call(pos, fill, x1, n_tiles)
    ys = _expert_call(tile_expert, n_active, xs, wg, wu, wd, layer)
    return _combine_call(pos, x1, route, ys, g, beta)


def _rope_tables(T):
    half = MLA_ROPE // 2
    pos = jnp.arange(T, dtype=F32)
    inv = ROPE_BASE ** (-jnp.arange(half, dtype=F32) / half)
    ang = pos[:, None] * inv[None, :]
    cos, sin = jnp.cos(ang), jnp.sin(ang)
    one = jnp.ones((T, MLA_NOPE), F32)
    z16 = jnp.zeros((T, half), F32)
    z32 = jnp.zeros((T, LANE - MLA_NOPE - MLA_ROPE), F32)
    z64 = jnp.zeros((T, MLA_NOPE), F32)
    rc = jnp.concatenate([one, cos, cos, z32], axis=1)
    rs1 = jnp.concatenate([z64, z16, sin, z32], axis=1)
    rs2 = jnp.concatenate([z64, -sin, z16, z32], axis=1)
    return rc, rs1, rs2


def _alibi_features(T):
    c = jnp.asarray(SLOPES, F32) * LOG2E
    c1 = c.astype(BF16).astype(F32)
    c2 = (c - c1).astype(BF16).astype(F32)
    c3 = (c - c1 - c2).astype(BF16).astype(F32)
    width = LANE - DSA_DIM
    qfeat = jnp.stack([c1, c1, c2, c2, c3, c3], axis=1)
    qfeat = jnp.pad(qfeat, ((0, 0), (0, width - qfeat.shape[1])))
    pos = jnp.arange(T, dtype=I32)
    hi = ((pos // 256) * 256).astype(F32)
    lo = (pos % 256).astype(F32)
    kfeat = jnp.stack([hi, lo, hi, lo, hi, lo], axis=1)
    kfeat = jnp.pad(kfeat, ((0, 0), (0, width - kfeat.shape[1])))
    return qfeat, kfeat


def _pack_w_in(w):
    sizes = (Q_LORA, KV_LORA, MLA_ROPE, DSA_HEADS * DSA_DIM, DSA_DIM, DSA_DIM, IDX_HEADS * IDX_DIM, IDX_DIM, IDX_HEADS)
    offs = np.concatenate([[0], np.cumsum(sizes)])
    w = w.astype(BF16)
    qa, kva, kr, dq, dk, dv, iq, ik, iw = [w[:, int(offs[j]):int(offs[j + 1])] for j in range(len(sizes))]
    z = lambda n: jnp.zeros((w.shape[0], n), w.dtype)
    cat = jnp.concatenate([
        qa, kva, dq, iq,
        z(MLA_NOPE), kr, z(LANE - MLA_NOPE - MLA_ROPE),
        dk, ik, z(LANE - DSA_DIM - IDX_DIM),
    ], axis=1)
    assert cat.shape[1] == C_END
    side = jnp.concatenate([dv, iw, z(R_END - DSA_DIM - IDX_HEADS)], axis=1).T
    return cat, side


def kernel(x, w_in, q_norm_g, w_q_up, kv_norm_g, w_uk, w_uv, w_o, ln1_g, ln1_b, router_w, router_bias,
           w_gate, w_up, w_down, ln2_g, ln2_b):
    B, T, D = x.shape
    rc, rs1, rs2 = _rope_tables(T)
    rwt = router_w.T
    rwh = rwt.astype(BF16)
    rwl = (rwt - rwh.astype(F32)).astype(BF16)
    rb = router_bias.reshape(N_EXPERTS, 1).astype(F32)
    qfeat, kfeat = _alibi_features(T)
    for l in range(DEPTH):
        wcat, wside = _pack_w_in(w_in[l])
        wq = w_q_up[l].astype(BF16).reshape(Q_LORA, MLA_HEADS, MLA_NOPE + MLA_ROPE)
        wq = jnp.pad(wq, ((0, 0), (0, 0), (0, HEAD_PAD - MLA_NOPE - MLA_ROPE)))
        wq = wq.reshape(Q_LORA, MLA_HEADS // 2, 2 * HEAD_PAD).transpose(1, 0, 2)
        wk = w_uk[l].astype(BF16).reshape(KV_LORA, MLA_HEADS, MLA_NOPE)
        wk = jnp.pad(wk, ((0, 0), (0, 0), (0, HEAD_PAD - MLA_NOPE)))
        wk = wk.reshape(KV_LORA, MLA_HEADS // 2, 2 * HEAD_PAD).transpose(1, 0, 2)
        wvt = w_uv[l].astype(BF16).T
        q, k, vt, dq, dk, dvt, iq, ik, iwt = _proj_call(
            x, wcat, wside, q_norm_g[l].reshape(1, Q_LORA), kv_norm_g[l].reshape(1, KV_LORA), wq, wk, wvt,
            rc, rs1, rs2, qfeat, kfeat)
        out_a = _mla_call(q, k, vt)
        out_b = _dsa_call(dq, iq, iwt, dk, dvt, ik)
        x1, route = _out_call(out_a, out_b, x, w_o[l].astype(BF16), ln1_g[l].reshape(1, D), ln1_b[l].reshape(1, D),
                              rwh, rwl, rb)
        y = _moe_call(x1.reshape(B * T, D), route.reshape(B * T, LANE), w_gate, w_up,
                      w_down, l, ln2_g[l].reshape(1, D), ln2_b[l].reshape(1, D))
        x = y.reshape(B, T, D)
    return x
```

```python
import jax
import jax.numpy as jnp
import numpy as np
from jax import lax
from jax.experimental import pallas as pl
from jax.experimental.pallas import tpu as pltpu

F32 = jnp.float32
BF16 = jnp.bfloat16
I32 = jnp.int32

D_MODEL = 1024
DEPTH = 2
MLA_HEADS = 8
MLA_NOPE = 64
MLA_ROPE = 32
MLA_V = 64
Q_LORA = 384
KV_LORA = 256
ROPE_BASE = 10000.0
DSA_HEADS = 8
DSA_DIM = 64
IDX_HEADS = 8
IDX_DIM = 32
TOPK_MAX = 256
N_EXPERTS = 16
N_GROUPS = 4
EXPERTS_PER_GROUP = 4
D_FF = 512
ALPHA = (2.0 * DEPTH) ** 0.25
LN_EPS = 1e-5
RMS_EPS = 1e-6
MLA_SCALE = (MLA_NOPE + MLA_ROPE) ** -0.5
LOG2E = 1.4426950408889634
SLOPES = tuple(2.0 ** (-8.0 * (h + 1) / DSA_HEADS) for h in range(DSA_HEADS))

V_AUG = 80
LANE = 128
SUBLANES = 8
HEAD_PAD = LANE
NEG = -1e30

C_QA = 0
C_KVA = C_QA + Q_LORA
C_DQ = C_KVA + KV_LORA
C_IQ = C_DQ + DSA_HEADS * DSA_DIM
C_KR = C_IQ + IDX_HEADS * IDX_DIM
C_KX = C_KR + LANE
C_END = C_KX + LANE
R_DV = 0
R_IW = DSA_DIM
R_END = LANE
IDX_SCALE = (IDX_DIM * IDX_HEADS) ** -0.5
DSA_SCALE = DSA_DIM ** -0.5

TM_PROJ = 512
TM_OUT = 1024
TM_OUT_SUB = 512
TQ_MLA = 256
TQ_DSA = 256
TK_DSA = 256
TOP_K = 2
TR_MOE = 512
TM_DISP = 512

NT_DIMS = (((1,), (1,)), ((), ()))


def _rms(x, g):
    return x * lax.rsqrt(jnp.mean(x * x, axis=-1, keepdims=True) + RMS_EPS) * g


def _layer_norm(y, g, b):
    mu = jnp.mean(y, axis=-1, keepdims=True)
    d = y - mu
    var = jnp.mean(d * d, axis=-1, keepdims=True)
    return d * lax.rsqrt(var + LN_EPS) * g + b


def _proj_kernel(x_ref, wcat_ref, wside_ref, qg_ref, kvg_ref, wq_ref, wk_ref, wv_ref, rc_ref, rs1_ref, rs2_ref,
                 qfeat_ref, kfeat_ref,
                 q_ref, k_ref, vt_ref, dq_ref, dk_ref, dvt_ref, iq_ref, ik_ref, iwt_ref):
    xb = x_ref[0].astype(BF16)
    proj = jnp.dot(xb, wcat_ref[...], preferred_element_type=F32)
    side = lax.dot_general(wside_ref[...], xb, NT_DIMS, preferred_element_type=F32)
    qn = _rms(proj[:, C_QA:C_QA + Q_LORA], qg_ref[...]).astype(BF16)
    ckv = _rms(proj[:, C_KVA:C_KVA + KV_LORA], kvg_ref[...]).astype(BF16)
    rc, rs1, rs2 = rc_ref[...], rs1_ref[...], rs2_ref[...]

    def rope(t):
        return t * rc + pltpu.roll(t, 16, 1) * rs1 + pltpu.roll(t, LANE - 16, 1) * rs2

    kr = rope(proj[:, C_KR:C_KR + LANE])
    tm = xb.shape[0]
    aug = jnp.concatenate([jnp.ones((1, tm), F32), jnp.zeros((V_AUG - MLA_V - 1, tm), F32)], axis=0)
    vt_all = lax.dot_general(wv_ref[...], ckv, NT_DIMS, preferred_element_type=F32)
    for h in range(MLA_HEADS):
        if h % 2 == 0:
            q2h = jnp.dot(qn, wq_ref[h // 2], preferred_element_type=F32)
            k2h = jnp.dot(ckv, wk_ref[h // 2], preferred_element_type=F32)
        half = slice(HEAD_PAD * (h % 2), HEAD_PAD * (h % 2 + 1))
        q_ref[0, h] = (rope(q2h[:, half]) * (MLA_SCALE * LOG2E)).astype(BF16)
        k_ref[0, h] = (k2h[:, half] + kr).astype(BF16)
        vth = jnp.concatenate([vt_all[MLA_V * h:MLA_V * (h + 1)], aug], axis=0)
        for j in range(tm // TQ_MLA):
            vt_ref[0, h, j] = vth[:, TQ_MLA * j:TQ_MLA * (j + 1)].astype(BF16)
    for h in range(DSA_HEADS):
        dqh = proj[:, C_DQ + DSA_DIM * h:C_DQ + DSA_DIM * (h + 1)] * (DSA_SCALE * LOG2E)
        feat = jnp.broadcast_to(qfeat_ref[h:h + 1, :], (tm, LANE - DSA_DIM))
        dq_ref[0, h] = jnp.concatenate([dqh, feat], axis=1).astype(BF16)
    for h in range(IDX_HEADS):
        iq_ref[0, h] = proj[:, C_IQ + IDX_DIM * h:C_IQ + IDX_DIM * (h + 1)].astype(BF16)
    dk_ref[0] = jnp.concatenate([proj[:, C_KX:C_KX + DSA_DIM], kfeat_ref[...]], axis=1).astype(BF16)
    ik_ref[0] = proj[:, C_KX + DSA_DIM:C_KX + DSA_DIM + IDX_DIM].astype(BF16)
    dvt = jnp.concatenate([side[R_DV:R_DV + DSA_DIM], aug], axis=0)
    for j in range(TM_PROJ // TK_DSA):
        dvt_ref[0, j] = dvt[:, TK_DSA * j:TK_DSA * (j + 1)].astype(BF16)
    iwt_ref[0] = side[R_IW:R_IW + IDX_HEADS, :] * IDX_SCALE


def _proj_call(x, wcat, wside, qg, kvg, wq, wk, wv, rc, rs1, rs2, qfeat, kfeat):
    B, T, _ = x.shape
    tm = TM_PROJ
    full = lambda shape: pl.BlockSpec(shape, lambda b, t: (0,) * len(shape))
    head_out = lambda d: pl.BlockSpec((1, MLA_HEADS, tm, d), lambda b, t: (b, 0, t, 0))
    tok_out = lambda d: pl.BlockSpec((1, tm, d), lambda b, t: (b, t, 0))
    tab = pl.BlockSpec((tm, LANE), lambda b, t: (t, 0))
    return pl.pallas_call(
        _proj_kernel,
        grid=(B, T // tm),
        in_specs=[
            pl.BlockSpec((1, tm, D_MODEL), lambda b, t: (b, t, 0)),
            full(wcat.shape), full(wside.shape), full(qg.shape), full(kvg.shape), full(wq.shape), full(wk.shape),
            full(wv.shape), tab, tab, tab, full(qfeat.shape),
            pl.BlockSpec((tm, LANE - DSA_DIM), lambda b, t: (t, 0)),
        ],
        out_specs=[head_out(HEAD_PAD), head_out(HEAD_PAD),
                   pl.BlockSpec((1, MLA_HEADS, tm // TQ_MLA, V_AUG, TQ_MLA), lambda b, t: (b, 0, t, 0, 0)),
                   head_out(LANE),
                   tok_out(LANE),
                   pl.BlockSpec((1, tm // TK_DSA, V_AUG, TK_DSA), lambda b, t: (b, t, 0, 0)),
                   head_out(IDX_DIM), tok_out(IDX_DIM),
                   pl.BlockSpec((1, IDX_HEADS, tm), lambda b, t: (b, 0, t))],
        out_shape=[
            jax.ShapeDtypeStruct((B, MLA_HEADS, T, HEAD_PAD), BF16),
            jax.ShapeDtypeStruct((B, MLA_HEADS, T, HEAD_PAD), BF16),
            jax.ShapeDtypeStruct((B, MLA_HEADS, T // TQ_MLA, V_AUG, TQ_MLA), BF16),
            jax.ShapeDtypeStruct((B, DSA_HEADS, T, LANE), BF16),
            jax.ShapeDtypeStruct((B, T, LANE), BF16),
            jax.ShapeDtypeStruct((B, T // TK_DSA, V_AUG, TK_DSA), BF16),
            jax.ShapeDtypeStruct((B, IDX_HEADS, T, IDX_DIM), BF16),
            jax.ShapeDtypeStruct((B, T, IDX_DIM), BF16),
            jax.ShapeDtypeStruct((B, IDX_HEADS, T), F32),
        ],
        compiler_params=pltpu.CompilerParams(dimension_semantics=("parallel", "parallel")),
        name="proj",
    )(x, wcat, wside, qg, kvg, wq, wk, wv, rc, rs1, rs2, qfeat, kfeat)


def _transposed_heads_store(o_ref, ot, heads, width, tq):
    for h in range(heads):
        o_ref[0, width * h:width * (h + 1), :] = ot[:, tq * h:tq * (h + 1)].astype(BF16)


def _mla_kernel(q_ref, k_ref, vt_ref, o_ref, m_ref, acc_ref, base_ref):
    i = pl.program_id(1)
    tq = TQ_MLA
    heads = MLA_HEADS
    m_ref[...] = jnp.full(m_ref.shape, NEG, F32)
    acc_ref[...] = jnp.zeros(acc_ref.shape, F32)
    key = lax.broadcasted_iota(I32, (tq, tq), 0)
    qpos = lax.broadcasted_iota(I32, (tq, tq), 1)
    base_ref[...] = jnp.where(key <= qpos, -NEG, NEG)

    def step(j, masked):
        start = pl.multiple_of(j * tq, tq)
        s_list = [lax.dot_general(k_ref[0, h, pl.ds(start, tq), :], q_ref[0, h], NT_DIMS,
                                  preferred_element_type=F32) for h in range(heads)]
        m_all, acc_all = m_ref[...], acc_ref[...]
        m_out, a_out, pv_out = [], [], []
        for h in range(heads):
            lanes = slice(tq * h, tq * (h + 1))
            s = jnp.minimum(s_list[h], base_ref[...]) if masked else s_list[h]
            m_prev = m_all[:, lanes]
            m_new = jnp.maximum(m_prev, jnp.max(s, axis=0, keepdims=True))
            p = jnp.exp2(s - m_new)
            pv_out.append(jnp.dot(vt_ref[0, h, j], p.astype(BF16), preferred_element_type=F32))
            a_out.append(jnp.exp2(m_prev - m_new))
            m_out.append(m_new)
        m_ref[...] = jnp.concatenate(m_out, axis=1)
        acc_ref[...] = jnp.concatenate(a_out, axis=1) * acc_all + jnp.concatenate(pv_out, axis=1)

    def body_pair(j, carry):
        step(2 * j, False)
        step(2 * j + 1, False)
        return carry

    def body_odd_tail(j, carry):
        step(i - 1, False)
        step(i, True)
        return carry

    def body_even_tail(j, carry):
        step(i, True)
        return carry

    odd = i & 1
    lax.fori_loop(0, lax.shift_right_logical(i, 1), body_pair, 0)
    lax.fori_loop(0, odd, body_odd_tail, 0)
    lax.fori_loop(0, 1 - odd, body_even_tail, 0)
    _transposed_heads_store(o_ref, acc_ref[0:MLA_V, :] / acc_ref[MLA_V:MLA_V + 1, :], heads, MLA_V, tq)


def _mla_call(q, k, vt):
    B, H, T, _ = q.shape
    tq = TQ_MLA
    return pl.pallas_call(
        _mla_kernel,
        grid=(B, T // tq),
        in_specs=[
            pl.BlockSpec((1, H, tq, HEAD_PAD), lambda b, i: (b, 0, i, 0)),
            pl.BlockSpec((1, H, T, HEAD_PAD), lambda b, i: (b, 0, 0, 0)),
            pl.BlockSpec((1, H, T // tq, V_AUG, tq), lambda b, i: (b, 0, 0, 0, 0)),
        ],
        out_specs=pl.BlockSpec((1, H * MLA_V, tq), lambda b, i: (b, 0, i)),
        out_shape=jax.ShapeDtypeStruct((B, H * MLA_V, T), BF16),
        scratch_shapes=[
            pltpu.VMEM((1, H * tq), F32),
            pltpu.VMEM((V_AUG, H * tq), F32),
            pltpu.VMEM((tq, tq), F32),
        ],
        compiler_params=pltpu.CompilerParams(dimension_semantics=("parallel", "arbitrary")),
        name="mla",
    )(q, k, vt)


def _dsa_kernel(dq_ref, iq_ref, iwt_ref, dk_ref, dvt_ref, ik_ref, o_ref, sc_ref, m_ref, acc_ref):
    i = pl.program_id(1)
    tq, tk = TQ_DSA, TK_DSA
    T = dk_ref.shape[1]
    nch = lax.div((i + 1) * tq + (tk - 1), tk)
    cols = DSA_HEADS * tq
    key = lax.broadcasted_iota(I32, (tk, tq), 0)
    t_q = i * tq + lax.broadcasted_iota(I32, (tk, tq), 1)
    t_row = i * tq + lax.broadcasted_iota(I32, (1, tq), 1)

    iq2 = iq_ref[0].reshape(cols, IDX_DIM)
    w8 = iwt_ref[0]

    def for_chunk_pairs(step):
        def pair_body(j, carry):
            step(2 * j)
            step(2 * j + 1)
            return carry

        lax.fori_loop(0, lax.shift_right_logical(nch, 1), pair_body, 0)

        @pl.when((nch & 1) == 1)
        def _():
            step(nch - 1)

    def score_step(c):
        start = pl.multiple_of(c * tk, tk)
        ikc = ik_ref[0, pl.ds(start, tk), :]
        rel = lax.dot_general(ikc, iq2, NT_DIMS, preferred_element_type=F32)
        sc = jnp.maximum(rel[:, 0:tq], 0.0) * w8[0:1]
        for h in range(1, IDX_HEADS):
            sc = sc + jnp.maximum(rel[:, tq * h:tq * (h + 1)], 0.0) * w8[h:h + 1]
        sc_ref[c] = jnp.where(key + c * tk <= t_q, sc, -jnp.inf)

    for_chunk_pairs(score_step)

    def count(pred):
        def hits(c):
            hit = jnp.where(pred(sc_ref[c], key + c * tk), 1.0, 0.0)
            return jnp.sum(hit.reshape(tk // SUBLANES, SUBLANES, tq), axis=0)

        def pair_body(j, acc):
            return acc + hits(2 * j) + hits(2 * j + 1)

        acc = lax.fori_loop(0, lax.shift_right_logical(nch, 1), pair_body, jnp.zeros((SUBLANES, tq), F32))
        acc = acc + lax.cond((nch & 1) == 1, lambda: hits(nch - 1), lambda: jnp.zeros((SUBLANES, tq), F32))
        return jnp.sum(acc, axis=0, keepdims=True)

    int_min = jnp.int32(-2 ** 31)

    def key_to_f32(ku):
        ks = ku ^ int_min
        bits = jnp.where(ks >= 0, ks, ks ^ jnp.int32(0x7FFFFFFF))
        return lax.bitcast_convert_type(bits, F32)

    kf = float(TOPK_MAX)

    def bis_body(step, carry):
        res, cnt_res = carry
        cand = res | lax.shift_left(jnp.int32(1), 31 - step)
        tau_c = key_to_f32(cand)
        cnt = count(lambda sc, idx: sc >= tau_c)
        ok = cnt >= kf
        return jnp.where(ok, cand, res), jnp.where(ok, cnt, cnt_res)

    bis_init = (jnp.zeros((1, tq), I32), jnp.full((1, tq), kf, F32))
    short = t_row < TOPK_MAX
    res, cnt_ge = lax.cond((i + 1) * tq > TOPK_MAX, lambda: lax.fori_loop(0, 32, bis_body, bis_init),
                           lambda: bis_init)
    tau = jnp.where(short, -jnp.inf, key_to_f32(res))
    cnt_ge = jnp.where(short, kf, cnt_ge)

    def tie_search():
        need = float(TOPK_MAX) - count(lambda sc, idx: sc > tau)

        def tie_body(step, jm):
            cand = jm | lax.shift_left(jnp.int32(1), 10 - step)
            cnt = count(lambda sc, idx: (sc >= tau) & jnp.logical_not(sc > tau) & (idx < cand))
            return jnp.where(cnt < need, cand, jm)

        return lax.fori_loop(0, 11, tie_body, jnp.zeros((1, tq), I32))

    j_max = lax.cond(jnp.max(cnt_ge) > float(TOPK_MAX), tie_search, lambda: jnp.full((1, tq), T, I32))

    q2 = dq_ref[0].reshape(cols, LANE)
    m_ref[...] = jnp.full((1, cols), NEG, F32)
    acc_ref[...] = jnp.zeros((V_AUG, cols), F32)

    def att_step(c):
        start = pl.multiple_of(c * tk, tk)
        kc = dk_ref[0, pl.ds(start, tk), :]
        vt = dvt_ref[0, c]
        sc = sc_ref[c]
        idx = key + c * tk
        sel = (sc >= tau) & ((sc > tau) | (idx <= j_max)) & (sc > -jnp.inf)
        cap = jnp.where(sel, -NEG, NEG)
        m_all, acc_all = m_ref[...], acc_ref[...]
        m_out, a_out, p_out = [], [], []
        s_all = lax.dot_general(kc, q2, NT_DIMS, preferred_element_type=F32)
        for h in range(DSA_HEADS):
            lanes = slice(tq * h, tq * (h + 1))
            s = jnp.minimum(s_all[:, lanes], cap)
            m_prev = m_all[:, lanes]
            m_new = jnp.maximum(m_prev, jnp.max(s, axis=0, keepdims=True))
            p_out.append(jnp.exp2(s - m_new).astype(BF16))
            a_out.append(jnp.exp2(m_prev - m_new))
            m_out.append(m_new)
        pv = jnp.dot(vt, jnp.concatenate(p_out, axis=1), preferred_element_type=F32)
        m_ref[...] = jnp.concatenate(m_out, axis=1)
        acc_ref[...] = jnp.concatenate(a_out, axis=1) * acc_all + pv

    for_chunk_pairs(att_step)
    _transposed_heads_store(o_ref, acc_ref[0:DSA_DIM, :] / acc_ref[DSA_DIM:DSA_DIM + 1, :], DSA_HEADS, DSA_DIM, tq)


def _dsa_call(dq, iq, iwt, dk, dvt, ik):
    B, H, T, _ = dq.shape
    tq, tk = TQ_DSA, TK_DSA
    cols = H * tq
    return pl.pallas_call(
        _dsa_kernel,
        grid=(B, T // tq),
        in_specs=[
            pl.BlockSpec((1, H, tq, LANE), lambda b, i: (b, 0, i, 0)),
            pl.BlockSpec((1, H, tq, IDX_DIM), lambda b, i: (b, 0, i, 0)),
            pl.BlockSpec((1, IDX_HEADS, tq), lambda b, i: (b, 0, i)),
            pl.BlockSpec((1, T, LANE), lambda b, i: (b, 0, 0)),
            pl.BlockSpec((1, T // tk, V_AUG, tk), lambda b, i: (b, 0, 0, 0)),
            pl.BlockSpec((1, T, IDX_DIM), lambda b, i: (b, 0, 0)),
        ],
        out_specs=pl.BlockSpec((1, H * DSA_DIM, tq), lambda b, i: (b, 0, i)),
        out_shape=jax.ShapeDtypeStruct((B, H * DSA_DIM, T), BF16),
        scratch_shapes=[
            pltpu.VMEM((T // tk, tk, tq), F32),
            pltpu.VMEM((1, cols), F32),
            pltpu.VMEM((V_AUG, cols), F32),
        ],
        compiler_params=pltpu.CompilerParams(dimension_semantics=("parallel", "arbitrary")),
        name="dsa",
    )(dq, iq, iwt, dk, dvt, ik)


def _route(aff, biased):
    b = [biased[e:e + 1] for e in range(N_EXPERTS)]
    a = [aff[e:e + 1] for e in range(N_EXPERTS)]
    gs = []
    for g in range(N_GROUPS):
        v0, v1, v2, v3 = b[4 * g:4 * g + 4]
        hi1, lo1 = jnp.maximum(v0, v1), jnp.minimum(v0, v1)
        hi2, lo2 = jnp.maximum(v2, v3), jnp.minimum(v2, v3)
        gs.append(jnp.maximum(hi1, hi2) + jnp.maximum(jnp.minimum(hi1, hi2), jnp.maximum(lo1, lo2)))
    best, sel = gs[0], jnp.zeros_like(gs[0], dtype=I32)
    for g in range(1, N_GROUPS):
        upd = gs[g] > best
        sel = jnp.where(upd, g, sel)
        best = jnp.where(upd, gs[g], best)
    v, av = [], []
    for k in range(EXPERTS_PER_GROUP):
        vk, ak = b[k], a[k]
        for g in range(1, N_GROUPS):
            vk = jnp.where(sel == g, b[4 * g + k], vk)
            ak = jnp.where(sel == g, a[4 * g + k], ak)
        v.append(vk)
        av.append(ak)
    m1, i1 = v[0], jnp.zeros_like(sel)
    for k in range(1, EXPERTS_PER_GROUP):
        upd = v[k] > m1
        i1 = jnp.where(upd, k, i1)
        m1 = jnp.where(upd, v[k], m1)
    m2, i2 = jnp.full_like(m1, -jnp.inf), jnp.zeros_like(sel)
    for k in range(EXPERTS_PER_GROUP):
        cand = jnp.where(i1 == k, -jnp.inf, v[k])
        upd = cand > m2
        i2 = jnp.where(upd, k, i2)
        m2 = jnp.where(upd, cand, m2)
    a1, a2 = jnp.zeros_like(m1), jnp.zeros_like(m1)
    for k in range(EXPERTS_PER_GROUP):
        a1 = jnp.where(i1 == k, av[k], a1)
        a2 = jnp.where(i2 == k, av[k], a2)
    tot = a1 + a2
    e1 = sel * EXPERTS_PER_GROUP + i1
    e2 = sel * EXPERTS_PER_GROUP + i2
    return e1, e2, a1 / tot, a2 / tot


def _out_kernel(a_ref, b_ref, x_ref, wo_ref, g_ref, beta_ref, rwh_ref, rwl_ref, rb_ref, x1_ref, route_ref):
    half = MLA_HEADS * MLA_V
    rwh, rwl = rwh_ref[...], rwl_ref[...]
    for s in range(TM_OUT // TM_OUT_SUB):
        rows = slice(TM_OUT_SUB * s, TM_OUT_SUB * (s + 1))
        tn = (((0,), (0,)), ((), ()))
        mix = lax.dot_general(a_ref[0, :, rows], wo_ref[0:half, :], tn, preferred_element_type=F32)
        mix = mix + lax.dot_general(b_ref[0, :, rows], wo_ref[half:2 * half, :], tn, preferred_element_type=F32)
        x1 = _layer_norm(ALPHA * x_ref[0, rows, :] + mix, g_ref[...], beta_ref[...])
        x1_ref[0, rows, :] = x1
        xh = x1.astype(BF16)
        xl = (x1 - xh.astype(F32)).astype(BF16)
        logits = lax.dot_general(rwh, xh, NT_DIMS, preferred_element_type=F32)
        logits = logits + lax.dot_general(rwh, xl, NT_DIMS, preferred_element_type=F32)
        logits = logits + lax.dot_general(rwl, xh, NT_DIMS, preferred_element_type=F32)
        aff = 1.0 / (1.0 + jnp.exp(-logits))
        e1, e2, w1, w2 = _route(aff, aff + rb_ref[...])
        info = [e1.astype(F32), e2.astype(F32), w1, w2, jnp.zeros((LANE - 4, TM_OUT_SUB), F32)]
        route_ref[0, rows, :] = jnp.concatenate(info, axis=0).T


def _out_call(a, b, x, wo, g, beta, rwh, rwl, rb):
    B, T, _ = x.shape
    tm = TM_OUT
    full = lambda shape: pl.BlockSpec(shape, lambda bb, t: (0,) * len(shape))
    tok = lambda d: pl.BlockSpec((1, tm, d), lambda bb, t: (bb, t, 0))
    return pl.pallas_call(
        _out_kernel,
        grid=(B, T // tm),
        in_specs=[pl.BlockSpec((1, a.shape[1], tm), lambda bb, t: (bb, 0, t)),
                  pl.BlockSpec((1, b.shape[1], tm), lambda bb, t: (bb, 0, t)),
                  tok(D_MODEL), full(wo.shape), full(g.shape), full(beta.shape),
                  full(rwh.shape), full(rwl.shape), full(rb.shape)],
        out_specs=[tok(D_MODEL), tok(LANE)],
        out_shape=[jax.ShapeDtypeStruct((B, T, D_MODEL), F32), jax.ShapeDtypeStruct((B, T, LANE), F32)],
        compiler_params=pltpu.CompilerParams(dimension_semantics=("parallel", "parallel")),
        name="outproj",
    )(a, b, x, wo, g, beta, rwh, rwl, rb)


def _route_positions(route, n_tiles):
    eid = route[:, 0:TOP_K].astype(I32).reshape(-1)
    onehot = (eid[:, None] == jnp.arange(N_EXPERTS, dtype=I32)[None, :]).astype(I32)
    csum = jnp.cumsum(onehot, axis=0)
    rank = jnp.sum(onehot * csum, axis=1) - 1
    counts = csum[-1]
    padded = ((counts + TR_MOE - 1) // TR_MOE) * TR_MOE
    ends = jnp.cumsum(padded)
    starts = ends - padded
    pos = jnp.sum(onehot * starts[None, :], axis=1) + rank
    n_active = ends[-1] // TR_MOE
    tile_row = jnp.minimum(jnp.arange(n_tiles, dtype=I32), n_active - 1) * TR_MOE
    tile_expert = jnp.sum((tile_row[:, None] >= ends[None, :]).astype(I32), axis=1)
    last_tile = jnp.where(padded > counts, ends // TR_MOE - 1, -1)
    spare = n_active + jnp.arange(N_EXPERTS, dtype=I32)
    fill = jnp.concatenate([last_tile, jnp.where(spare < n_tiles, spare, -1)])
    return pos.astype(I32), tile_expert.astype(I32), n_active.reshape(1).astype(I32), fill.astype(I32)


def _row_copy_loop(tm, copy):
    def body(q, carry):
        base = pl.multiple_of(q * SUBLANES, SUBLANES)
        for j in range(SUBLANES):
            for k in range(TOP_K):
                copy(base + j, (base + j) * TOP_K + k, k).start(priority=k)
        return carry

    lax.fori_loop(0, tm // SUBLANES, body, 0)


def _dispatch_kernel(pos_ref, fill_ref, x_ref, xs_ref, xr_ref, sem, zsem):
    tm = x_ref.shape[0]

    @pl.when(pl.program_id(0) == 0)
    def _():
        xr_ref[...] = jnp.zeros(xr_ref.shape, F32)

        def zero_copy(j):
            start = pl.multiple_of(jnp.maximum(fill_ref[j], 0) * tm, tm)
            return pltpu.make_async_copy(xr_ref, xs_ref.at[pl.ds(start, tm)], zsem)

        for j in range(fill_ref.shape[0]):
            @pl.when(fill_ref[j] >= 0)
            def _():
                zero_copy(j).start()

        for j in range(fill_ref.shape[0]):
            @pl.when(fill_ref[j] >= 0)
            def _():
                zero_copy(j).wait()

    xr_ref[...] = x_ref[...].reshape(tm, SUBLANES, LANE)
    _row_copy_loop(tm, lambda r, p, k: pltpu.make_async_copy(xr_ref.at[r], xs_ref.at[pos_ref[p]], sem))
    for k in range(TOP_K):
        pltpu.make_async_copy(xr_ref, xs_ref.at[pl.ds(0, tm)], sem).wait()


def _dispatch_call(pos, fill, x1, n_tiles):
    n = x1.shape[0]
    tm = TM_DISP
    assert tm == TR_MOE
    return pl.pallas_call(
        _dispatch_kernel,
        grid=(n // tm,),
        in_specs=[
            pl.BlockSpec((TOP_K * tm,), lambda i: (i,), memory_space=pltpu.SMEM),
            pl.BlockSpec(fill.shape, lambda i: (0,), memory_space=pltpu.SMEM),
            pl.BlockSpec((tm, D_MODEL), lambda i: (i, 0)),
        ],
        out_specs=pl.BlockSpec(memory_space=pl.ANY),
        out_shape=jax.ShapeDtypeStruct((n_tiles * TR_MOE, SUBLANES, LANE), F32),
        scratch_shapes=[pltpu.VMEM((tm, SUBLANES, LANE), F32), pltpu.SemaphoreType.DMA(()),
                        pltpu.SemaphoreType.DMA(())],
        compiler_params=pltpu.CompilerParams(dimension_semantics=("arbitrary",)),
        name="dispatch",
    )(pos, fill, x1)


def _expert_kernel(te_ref, na_ref, xs_ref, wg_ref, wu_ref, wd_ref, ys_ref):
    del te_ref
    active = pl.program_id(0) < na_ref[0]

    @pl.when(active)
    def _():
        tr = xs_ref.shape[0]
        xb = xs_ref[...].reshape(tr, D_MODEL).astype(BF16)
        hg = jnp.dot(xb, wg_ref[0, 0].astype(BF16), preferred_element_type=F32)
        hu = jnp.dot(xb, wu_ref[0, 0].astype(BF16), preferred_element_type=F32)
        act = (hg / (1.0 + jnp.exp(-hg))) * hu
        y = jnp.dot(act.astype(BF16), wd_ref[0, 0].astype(BF16), preferred_element_type=F32)
        ys_ref[...] = y.reshape(tr, SUBLANES, LANE)

    @pl.when(jnp.logical_not(active))
    def _():
        ys_ref[...] = jnp.zeros_like(ys_ref)


def _expert_call(tile_expert, n_active, xs, wg, wu, wd, layer):
    rows = xs.shape[0]
    tr = TR_MOE
    row_map = lambda i, te, na: (jnp.minimum(i, na[0] - 1), 0, 0)
    out_map = lambda i, te, na: (i, 0, 0)
    w_map = lambda i, te, na: (layer, te[i], 0, 0)
    return pl.pallas_call(
        _expert_kernel,
        grid_spec=pltpu.PrefetchScalarGridSpec(
            num_scalar_prefetch=2,
            grid=(rows // tr,),
            in_specs=[
                pl.BlockSpec((tr, SUBLANES, LANE), row_map),
                pl.BlockSpec((1, 1, D_MODEL, D_FF), w_map),
                pl.BlockSpec((1, 1, D_MODEL, D_FF), w_map),
                pl.BlockSpec((1, 1, D_FF, D_MODEL), w_map),
            ],
            out_specs=pl.BlockSpec((tr, SUBLANES, LANE), out_map),
        ),
        out_shape=jax.ShapeDtypeStruct(xs.shape, F32),
        compiler_params=pltpu.CompilerParams(dimension_semantics=("arbitrary",)),
        name="experts",
    )(tile_expert, n_active, xs, wg, wu, wd)


def _combine_kernel(pos_ref, posn_ref, x1_ref, route_ref, ys_ref, g_ref, beta_ref, o_ref, buf_a, buf_b, sem_a, sem_b):
    i = pl.program_id(0)
    n_steps = pl.num_programs(0)
    tm = x1_ref.shape[0]

    def row_copy(idx_ref, p, buf, k, r, sem):
        return pltpu.make_async_copy(ys_ref.at[idx_ref[p]], buf.at[k, r], sem)

    def wait_tile(buf, sem):
        for k in range(TOP_K):
            pltpu.make_async_copy(ys_ref.at[pl.ds(0, tm)], buf.at[k], sem).wait()

    @pl.when(i == 0)
    def _():
        _row_copy_loop(tm, lambda r, p, k: row_copy(pos_ref, p, buf_a, k, r, sem_a))

    def step(cur, sem_cur, nxt, sem_nxt):
        wait_tile(cur, sem_cur)
        for r in range(tm):
            for k in range(TOP_K):
                row_copy(posn_ref, TOP_K * r + k, nxt, k, r, sem_nxt).start(priority=k)
        rt = route_ref[...]
        ffn = rt[:, TOP_K:TOP_K + 1] * cur[0].reshape(tm, D_MODEL)
        for k in range(1, TOP_K):
            ffn = ffn + rt[:, TOP_K + k:TOP_K + k + 1] * cur[k].reshape(tm, D_MODEL)
        o_ref[...] = _layer_norm(ALPHA * x1_ref[...] + ffn, g_ref[...], beta_ref[...])

        @pl.when(i == n_steps - 1)
        def _():
            wait_tile(nxt, sem_nxt)

    @pl.when((i & 1) == 0)
    def _():
        step(buf_a, sem_a, buf_b, sem_b)

    @pl.when((i & 1) == 1)
    def _():
        step(buf_b, sem_b, buf_a, sem_a)


def _combine_call(pos, x1, route, ys, g, beta):
    n = x1.shape[0]
    tm = TM_DISP
    n_steps = n // tm
    vec = pl.BlockSpec((1, D_MODEL), lambda i: (0, 0))
    buf = pltpu.VMEM((TOP_K, tm, SUBLANES, LANE), F32)
    return pl.pallas_call(
        _combine_kernel,
        grid=(n_steps,),
        in_specs=[
            pl.BlockSpec((TOP_K * tm,), lambda i: (i,), memory_space=pltpu.SMEM),
            pl.BlockSpec((TOP_K * tm,), lambda i: (jnp.minimum(i + 1, n_steps - 1),), memory_space=pltpu.SMEM),
            pl.BlockSpec((tm, D_MODEL), lambda i: (i, 0)),
            pl.BlockSpec((tm, LANE), lambda i: (i, 0)),
            pl.BlockSpec(memory_space=pl.ANY),
            vec, vec,
        ],
        out_specs=pl.BlockSpec((tm, D_MODEL), lambda i: (i, 0)),
        out_shape=jax.ShapeDtypeStruct((n, D_MODEL), F32),
        scratch_shapes=[buf, buf, pltpu.SemaphoreType.DMA(()), pltpu.SemaphoreType.DMA(())],
        compiler_params=pltpu.CompilerParams(dimension_semantics=("arbitrary",)),
        name="combine",
    )(pos, pos, x1, route, ys, g, beta)


def _moe_call(x1, route, wg, wu, wd, layer, g, beta):
    n = x1.shape[0]
    n_tiles = TOP_K * n // TR_MOE + N_EXPERTS
    pos, tile_expert, n_active, fill = _route_positions(route, n_tiles)
    xs = _dispatch_call(pos, fill, x1, n_tiles)
    ys = _expert_call(tile_expert, n_active, xs, wg, wu, wd, layer)
    return _combine_call(pos, x1, route, ys, g, beta)


def _rope_tables(T):
    half = MLA_ROPE // 2
    pos = jnp.arange(T, dtype=F32)
    inv = ROPE_BASE ** (-jnp.arange(half, dtype=F32) / half)
    ang = pos[:, None] * inv[None, :]
    cos, sin = jnp.cos(ang), jnp.sin(ang)
    one = jnp.ones((T, MLA_NOPE), F32)
    z16 = jnp.zeros((T, half), F32)
    z32 = jnp.zeros((T, LANE - MLA_NOPE - MLA_ROPE), F32)
    z64 = jnp.zeros((T, MLA_NOPE), F32)
    rc = jnp.concatenate([one, cos, cos, z32], axis=1)
    rs1 = jnp.concatenate([z64, z16, sin, z32], axis=1)
    rs2 = jnp.concatenate([z64, -sin, z16, z32], axis=1)
    return rc, rs1, rs2


def _alibi_features(T):
    c = jnp.asarray(SLOPES, F32) * LOG2E
    c1 = c.astype(BF16).astype(F32)
    c2 = (c - c1).astype(BF16).astype(F32)
    c3 = (c - c1 - c2).astype(BF16).astype(F32)
    width = LANE - DSA_DIM
    qfeat = jnp.stack([c1, c1, c2, c2, c3, c3], axis=1)
    qfeat = jnp.pad(qfeat, ((0, 0), (0, width - qfeat.shape[1])))
    pos = jnp.arange(T, dtype=I32)
    hi = ((pos // 256) * 256).astype(F32)
    lo = (pos % 256).astype(F32)
    kfeat = jnp.stack([hi, lo, hi, lo, hi, lo], axis=1)
    kfeat = jnp.pad(kfeat, ((0, 0), (0, width - kfeat.shape[1])))
    return qfeat, kfeat


def _pack_w_in(w):
    sizes = (Q_LORA, KV_LORA, MLA_ROPE, DSA_HEADS * DSA_DIM, DSA_DIM, DSA_DIM, IDX_HEADS * IDX_DIM, IDX_DIM, IDX_HEADS)
    offs = np.concatenate([[0], np.cumsum(sizes)])
    w = w.astype(BF16)
    qa, kva, kr, dq, dk, dv, iq, ik, iw = [w[:, int(offs[j]):int(offs[j + 1])] for j in range(len(sizes))]
    z = lambda n: jnp.zeros((w.shape[0], n), w.dtype)
    cat = jnp.concatenate([
        qa, kva, dq, iq,
        z(MLA_NOPE), kr, z(LANE - MLA_NOPE - MLA_ROPE),
        dk, ik, z(LANE - DSA_DIM - IDX_DIM),
    ], axis=1)
    assert cat.shape[1] == C_END
    side = jnp.concatenate([dv, iw, z(R_END - DSA_DIM - IDX_HEADS)], axis=1).T
    return cat, side


def kernel(x, w_in, q_norm_g, w_q_up, kv_norm_g, w_uk, w_uv, w_o, ln1_g, ln1_b, router_w, router_bias,
           w_gate, w_up, w_down, ln2_g, ln2_b):
    B, T, D = x.shape
    rc, rs1, rs2 = _rope_tables(T)
    rwt = router_w.T
    rwh = rwt.astype(BF16)
    rwl = (rwt - rwh.astype(F32)).astype(BF16)
    rb = router_bias.reshape(N_EXPERTS, 1).astype(F32)
    qfeat, kfeat = _alibi_features(T)
    for l in range(DEPTH):
        wcat, wside = _pack_w_in(w_in[l])
        wq = w_q_up[l].astype(BF16).reshape(Q_LORA, MLA_HEADS, MLA_NOPE + MLA_ROPE)
        wq = jnp.pad(wq, ((0, 0), (0, 0), (0, HEAD_PAD - MLA_NOPE - MLA_ROPE)))
        wq = wq.reshape(Q_LORA, MLA_HEADS // 2, 2 * HEAD_PAD).transpose(1, 0, 2)
        wk = w_uk[l].astype(BF16).reshape(KV_LORA, MLA_HEADS, MLA_NOPE)
        wk = jnp.pad(wk, ((0, 0), (0, 0), (0, HEAD_PAD - MLA_NOPE)))
        wk = wk.reshape(KV_LORA, MLA_HEADS // 2, 2 * HEAD_PAD).transpose(1, 0, 2)
        wvt = w_uv[l].astype(BF16).T
        q, k, vt, dq, dk, dvt, iq, ik, iwt = _proj_call(
            x, wcat, wside, q_norm_g[l].reshape(1, Q_LORA), kv_norm_g[l].reshape(1, KV_LORA), wq, wk, wvt,
            rc, rs1, rs2, qfeat, kfeat)
        out_a = _mla_call(q, k, vt)
        out_b = _dsa_call(dq, iq, iwt, dk, dvt, ik)
        x1, route = _out_call(out_a, out_b, x, w_o[l].astype(BF16), ln1_g[l].reshape(1, D), ln1_b[l].reshape(1, D),
                              rwh, rwl, rb)
        y = _moe_call(x1.reshape(B * T, D), route.reshape(B * T, LANE), w_gate, w_up,
                      w_down, l, ln2_g[l].reshape(1, D), ln2_b[l].reshape(1, D))
        x = y.reshape(B, T, D)
    return x
```

```python
import jax
import jax.numpy as jnp
import numpy as np
from jax import lax
from jax.experimental import pallas as pl
from jax.experimental.pallas import tpu as pltpu

F32 = jnp.float32
BF16 = jnp.bfloat16
I32 = jnp.int32

D_MODEL = 1024
DEPTH = 2
MLA_HEADS = 8
MLA_NOPE = 64
MLA_ROPE = 32
MLA_V = 64
Q_LORA = 384
KV_LORA = 256
ROPE_BASE = 10000.0
DSA_HEADS = 8
DSA_DIM = 64
IDX_HEADS = 8
IDX_DIM = 32
TOPK_MAX = 256
N_EXPERTS = 16
N_GROUPS = 4
EXPERTS_PER_GROUP = 4
D_FF = 512
ALPHA = (2.0 * DEPTH) ** 0.25
LN_EPS = 1e-5
RMS_EPS = 1e-6
MLA_SCALE = (MLA_NOPE + MLA_ROPE) ** -0.5
LOG2E = 1.4426950408889634
SLOPES = tuple(2.0 ** (-8.0 * (h + 1) / DSA_HEADS) for h in range(DSA_HEADS))

V_AUG = 80
LANE = 128
SUBLANES = 8
HEAD_PAD = LANE
NEG = -1e30

C_QA = 0
C_KVA = C_QA + Q_LORA
C_DQ = C_KVA + KV_LORA
C_IQ = C_DQ + DSA_HEADS * DSA_DIM
C_KR = C_IQ + IDX_HEADS * IDX_DIM
C_KX = C_KR + LANE
C_END = C_KX + LANE
R_DV = 0
R_IW = DSA_DIM
R_END = LANE
IDX_SCALE = (IDX_DIM * IDX_HEADS) ** -0.5
DSA_SCALE = DSA_DIM ** -0.5

TM_PROJ = 512
TM_OUT = 1024
TM_OUT_SUB = 512
TQ_MLA = 256
TQ_DSA = 256
TK_DSA = 256
TOP_K = 2
TR_MOE = 512
TM_DISP = 512

NT_DIMS = (((1,), (1,)), ((), ()))


def _rms(x, g):
    return x * lax.rsqrt(jnp.mean(x * x, axis=-1, keepdims=True) + RMS_EPS) * g


def _layer_norm(y, g, b):
    mu = jnp.mean(y, axis=-1, keepdims=True)
    d = y - mu
    var = jnp.mean(d * d, axis=-1, keepdims=True)
    return d * lax.rsqrt(var + LN_EPS) * g + b


def _proj_kernel(x_ref, wcat_ref, wside_ref, qg_ref, kvg_ref, wq_ref, wk_ref, wv_ref, rc_ref, rs1_ref, rs2_ref,
                 qfeat_ref, kfeat_ref,
                 q_ref, k_ref, vt_ref, dq_ref, dk_ref, dvt_ref, iq_ref, ik_ref, iwt_ref):
    xb = x_ref[0].astype(BF16)
    proj = jnp.dot(xb, wcat_ref[...], preferred_element_type=F32)
    side = lax.dot_general(wside_ref[...], xb, NT_DIMS, preferred_element_type=F32)
    qn = _rms(proj[:, C_QA:C_QA + Q_LORA], qg_ref[...]).astype(BF16)
    ckv = _rms(proj[:, C_KVA:C_KVA + KV_LORA], kvg_ref[...]).astype(BF16)
    rc, rs1, rs2 = rc_ref[...], rs1_ref[...], rs2_ref[...]

    def rope(t):
        return t * rc + pltpu.roll(t, 16, 1) * rs1 + pltpu.roll(t, LANE - 16, 1) * rs2

    kr = rope(proj[:, C_KR:C_KR + LANE])
    tm = xb.shape[0]
    aug = jnp.concatenate([jnp.ones((1, tm), F32), jnp.zeros((V_AUG - MLA_V - 1, tm), F32)], axis=0)
    vt_all = lax.dot_general(wv_ref[...], ckv, NT_DIMS, preferred_element_type=F32)
    for h in range(MLA_HEADS):
        if h % 2 == 0:
            q2h = jnp.dot(qn, wq_ref[h // 2], preferred_element_type=F32)
            k2h = jnp.dot(ckv, wk_ref[h // 2], preferred_element_type=F32)
        half = slice(HEAD_PAD * (h % 2), HEAD_PAD * (h % 2 + 1))
        q_ref[0, h] = (rope(q2h[:, half]) * (MLA_SCALE * LOG2E)).astype(BF16)
        k_ref[0, h] = (k2h[:, half] + kr).astype(BF16)
        vth = jnp.concatenate([vt_all[MLA_V * h:MLA_V * (h + 1)], aug], axis=0)
        for j in range(tm // TQ_MLA):
            vt_ref[0, h, j] = vth[:, TQ_MLA * j:TQ_MLA * (j + 1)].astype(BF16)
    for h in range(DSA_HEADS):
        dqh = proj[:, C_DQ + DSA_DIM * h:C_DQ + DSA_DIM * (h + 1)] * (DSA_SCALE * LOG2E)
        feat = jnp.broadcast_to(qfeat_ref[h:h + 1, :], (tm, LANE - DSA_DIM))
        dq_ref[0, h] = jnp.concatenate([dqh, feat], axis=1).astype(BF16)
    for h in range(IDX_HEADS):
        iq_ref[0, h] = proj[:, C_IQ + IDX_DIM * h:C_IQ + IDX_DIM * (h + 1)].astype(BF16)
    dk_ref[0] = jnp.concatenate([proj[:, C_KX:C_KX + DSA_DIM], kfeat_ref[...]], axis=1).astype(BF16)
    ik_ref[0] = proj[:, C_KX + DSA_DIM:C_KX + DSA_DIM + IDX_DIM].astype(BF16)
    dvt = jnp.concatenate([side[R_DV:R_DV + DSA_DIM], aug], axis=0)
    for j in range(TM_PROJ // TK_DSA):
        dvt_ref[0, j] = dvt[:, TK_DSA * j:TK_DSA * (j + 1)].astype(BF16)
    iwt_ref[0] = side[R_IW:R_IW + IDX_HEADS, :] * IDX_SCALE


def _proj_call(x, wcat, wside, qg, kvg, wq, wk, wv, rc, rs1, rs2, qfeat, kfeat):
    B, T, _ = x.shape
    tm = TM_PROJ
    full = lambda shape: pl.BlockSpec(shape, lambda b, t: (0,) * len(shape))
    head_out = lambda d: pl.BlockSpec((1, MLA_HEADS, tm, d), lambda b, t: (b, 0, t, 0))
    tok_out = lambda d: pl.BlockSpec((1, tm, d), lambda b, t: (b, t, 0))
    tab = pl.BlockSpec((tm, LANE), lambda b, t: (t, 0))
    return pl.pallas_call(
        _proj_kernel,
        grid=(B, T // tm),
        in_specs=[
            pl.BlockSpec((1, tm, D_MODEL), lambda b, t: (b, t, 0)),
            full(wcat.shape), full(wside.shape), full(qg.shape), full(kvg.shape), full(wq.shape), full(wk.shape),
            full(wv.shape), tab, tab, tab, full(qfeat.shape),
            pl.BlockSpec((tm, LANE - DSA_DIM), lambda b, t: (t, 0)),
        ],
        out_specs=[head_out(HEAD_PAD), head_out(HEAD_PAD),
                   pl.BlockSpec((1, MLA_HEADS, tm // TQ_MLA, V_AUG, TQ_MLA), lambda b, t: (b, 0, t, 0, 0)),
                   head_out(LANE),
                   tok_out(LANE),
                   pl.BlockSpec((1, tm // TK_DSA, V_AUG, TK_DSA), lambda b, t: (b, t, 0, 0)),
                   head_out(IDX_DIM), tok_out(IDX_DIM),
                   pl.BlockSpec((1, IDX_HEADS, tm), lambda b, t: (b, 0, t))],
        out_shape=[
            jax.ShapeDtypeStruct((B, MLA_HEADS, T, HEAD_PAD), BF16),
            jax.ShapeDtypeStruct((B, MLA_HEADS, T, HEAD_PAD), BF16),
            jax.ShapeDtypeStruct((B, MLA_HEADS, T // TQ_MLA, V_AUG, TQ_MLA), BF16),
            jax.ShapeDtypeStruct((B, DSA_HEADS, T, LANE), BF16),
            jax.ShapeDtypeStruct((B, T, LANE), BF16),
            jax.ShapeDtypeStruct((B, T // TK_DSA, V_AUG, TK_DSA), BF16),
            jax.ShapeDtypeStruct((B, IDX_HEADS, T, IDX_DIM), BF16),
            jax.ShapeDtypeStruct((B, T, IDX_DIM), BF16),
            jax.ShapeDtypeStruct((B, IDX_HEADS, T), F32),
        ],
        compiler_params=pltpu.CompilerParams(dimension_semantics=("parallel", "parallel")),
        name="proj",
    )(x, wcat, wside, qg, kvg, wq, wk, wv, rc, rs1, rs2, qfeat, kfeat)


def _transposed_heads_store(o_ref, ot, heads, width, tq):
    for h in range(heads):
        o_ref[0, width * h:width * (h + 1), :] = ot[:, tq * h:tq * (h + 1)].astype(BF16)


def _mla_kernel(q_ref, k_ref, vt_ref, o_ref, m_ref, acc_ref, base_ref):
    i = pl.program_id(1)
    tq = TQ_MLA
    heads = MLA_HEADS
    m_ref[...] = jnp.full(m_ref.shape, NEG, F32)
    acc_ref[...] = jnp.zeros(acc_ref.shape, F32)
    key = lax.broadcasted_iota(I32, (tq, tq), 0)
    qpos = lax.broadcasted_iota(I32, (tq, tq), 1)
    base_ref[...] = jnp.where(key <= qpos, -NEG, NEG)

    def step(j, masked):
        start = pl.multiple_of(j * tq, tq)
        s_list = [lax.dot_general(k_ref[0, h, pl.ds(start, tq), :], q_ref[0, h], NT_DIMS,
                                  preferred_element_type=F32) for h in range(heads)]
        m_all, acc_all = m_ref[...], acc_ref[...]
        m_out, a_out, pv_out = [], [], []
        for h in range(heads):
            lanes = slice(tq * h, tq * (h + 1))
            s = jnp.minimum(s_list[h], base_ref[...]) if masked else s_list[h]
            m_prev = m_all[:, lanes]
            m_new = jnp.maximum(m_prev, jnp.max(s, axis=0, keepdims=True))
            p = jnp.exp2(s - m_new)
            pv_out.append(jnp.dot(vt_ref[0, h, j], p.astype(BF16), preferred_element_type=F32))
            a_out.append(jnp.exp2(m_prev - m_new))
            m_out.append(m_new)
        m_ref[...] = jnp.concatenate(m_out, axis=1)
        acc_ref[...] = jnp.concatenate(a_out, axis=1) * acc_all + jnp.concatenate(pv_out, axis=1)

    def body_pair(j, carry):
        step(2 * j, False)
        step(2 * j + 1, False)
        return carry

    def body_odd_tail(j, carry):
        step(i - 1, False)
        step(i, True)
        return carry

    def body_even_tail(j, carry):
        step(i, True)
        return carry

    odd = i & 1
    lax.fori_loop(0, lax.shift_right_logical(i, 1), body_pair, 0)
    lax.fori_loop(0, odd, body_odd_tail, 0)
    lax.fori_loop(0, 1 - odd, body_even_tail, 0)
    _transposed_heads_store(o_ref, acc_ref[0:MLA_V, :] / acc_ref[MLA_V:MLA_V + 1, :], heads, MLA_V, tq)


def _mla_call(q, k, vt):
    B, H, T, _ = q.shape
    tq = TQ_MLA
    return pl.pallas_call(
        _mla_kernel,
        grid=(B, T // tq),
        in_specs=[
            pl.BlockSpec((1, H, tq, HEAD_PAD), lambda b, i: (b, 0, i, 0)),
            pl.BlockSpec((1, H, T, HEAD_PAD), lambda b, i: (b, 0, 0, 0)),
            pl.BlockSpec((1, H, T // tq, V_AUG, tq), lambda b, i: (b, 0, 0, 0, 0)),
        ],
        out_specs=pl.BlockSpec((1, H * MLA_V, tq), lambda b, i: (b, 0, i)),
        out_shape=jax.ShapeDtypeStruct((B, H * MLA_V, T), BF16),
        scratch_shapes=[
            pltpu.VMEM((1, H * tq), F32),
            pltpu.VMEM((V_AUG, H * tq), F32),
            pltpu.VMEM((tq, tq), F32),
        ],
        compiler_params=pltpu.CompilerParams(dimension_semantics=("parallel", "arbitrary")),
        name="mla",
    )(q, k, vt)


def _dsa_kernel(dq_ref, iq_ref, iwt_ref, dk_ref, dvt_ref, ik_ref, o_ref, sc_ref, m_ref, acc_ref):
    i = pl.program_id(1)
    tq, tk = TQ_DSA, TK_DSA
    T = dk_ref.shape[1]
    nch = lax.div((i + 1) * tq + (tk - 1), tk)
    cols = DSA_HEADS * tq
    key = lax.broadcasted_iota(I32, (tk, tq), 0)
    t_q = i * tq + lax.broadcasted_iota(I32, (tk, tq), 1)
    t_row = i * tq + lax.broadcasted_iota(I32, (1, tq), 1)

    iq2 = iq_ref[0].reshape(cols, IDX_DIM)
    w8 = iwt_ref[0]

    def for_chunk_pairs(step):
        def pair_body(j, carry):
            step(2 * j)
            step(2 * j + 1)
            return carry

        lax.fori_loop(0, lax.shift_right_logical(nch, 1), pair_body, 0)

        @pl.when((nch & 1) == 1)
        def _():
            step(nch - 1)

    def score_step(c):
        start = pl.multiple_of(c * tk, tk)
        ikc = ik_ref[0, pl.ds(start, tk), :]
        rel = lax.dot_general(ikc, iq2, NT_DIMS, preferred_element_type=F32)
        sc = jnp.maximum(rel[:, 0:tq], 0.0) * w8[0:1]
        for h in range(1, IDX_HEADS):
            sc = sc + jnp.maximum(rel[:, tq * h:tq * (h + 1)], 0.0) * w8[h:h + 1]
        sc_ref[c] = jnp.where(key + c * tk <= t_q, sc, -jnp.inf)

    for_chunk_pairs(score_step)

    def count(pred):
        def hits(c):
            hit = jnp.where(pred(sc_ref[c], key + c * tk), 1.0, 0.0)
            return jnp.sum(hit.reshape(tk // SUBLANES, SUBLANES, tq), axis=0)

        def pair_body(j, acc):
            return acc + hits(2 * j) + hits(2 * j + 1)

        acc = lax.fori_loop(0, lax.shift_right_logical(nch, 1), pair_body, jnp.zeros((SUBLANES, tq), F32))
        acc = acc + lax.cond((nch & 1) == 1, lambda: hits(nch - 1), lambda: jnp.zeros((SUBLANES, tq), F32))
        return jnp.sum(acc, axis=0, keepdims=True)

    int_min = jnp.int32(-2 ** 31)

    def key_to_f32(ku):
        ks = ku ^ int_min
        bits = jnp.where(ks >= 0, ks, ks ^ jnp.int32(0x7FFFFFFF))
        return lax.bitcast_convert_type(bits, F32)

    kf = float(TOPK_MAX)

    def bis_body(step, carry):
        res, cnt_res = carry
        cand = res | lax.shift_left(jnp.int32(1), 31 - step)
        tau_c = key_to_f32(cand)
        cnt = count(lambda sc, idx: sc >= tau_c)
        ok = cnt >= kf
        return jnp.where(ok, cand, res), jnp.where(ok, cnt, cnt_res)

    bis_init = (jnp.zeros((1, tq), I32), jnp.full((1, tq), kf, F32))
    short = t_row < TOPK_MAX
    res, cnt_ge = lax.cond((i + 1) * tq > TOPK_MAX, lambda: lax.fori_loop(0, 32, bis_body, bis_init),
                           lambda: bis_init)
    tau = jnp.where(short, -jnp.inf, key_to_f32(res))
    cnt_ge = jnp.where(short, kf, cnt_ge)

    def tie_search():
        need = float(TOPK_MAX) - count(lambda sc, idx: sc > tau)

        def tie_body(step, jm):
            cand = jm | lax.shift_left(jnp.int32(1), 10 - step)
            cnt = count(lambda sc, idx: (sc >= tau) & jnp.logical_not(sc > tau) & (idx < cand))
            return jnp.where(cnt < need, cand, jm)

        return lax.fori_loop(0, 11, tie_body, jnp.zeros((1, tq), I32))

    j_max = lax.cond(jnp.max(cnt_ge) > float(TOPK_MAX), tie_search, lambda: jnp.full((1, tq), T, I32))

    q2 = dq_ref[0].reshape(cols, LANE)
    m_ref[...] = jnp.full((1, cols), NEG, F32)
    acc_ref[...] = jnp.zeros((V_AUG, cols), F32)

    def att_step(c):
        start = pl.multiple_of(c * tk, tk)
        kc = dk_ref[0, pl.ds(start, tk), :]
        vt = dvt_ref[0, c]
        sc = sc_ref[c]
        idx = key + c * tk
        sel = (sc >= tau) & ((sc > tau) | (idx <= j_max)) & (sc > -jnp.inf)
        cap = jnp.where(sel, -NEG, NEG)
        m_all, acc_all = m_ref[...], acc_ref[...]
        m_out, a_out, p_out = [], [], []
        s_all = lax.dot_general(kc, q2, NT_DIMS, preferred_element_type=F32)
        for h in range(DSA_HEADS):
            lanes = slice(tq * h, tq * (h + 1))
            s = jnp.minimum(s_all[:, lanes], cap)
            m_prev = m_all[:, lanes]
            m_new = jnp.maximum(m_prev, jnp.max(s, axis=0, keepdims=True))
            p_out.append(jnp.exp2(s - m_new).astype(BF16))
            a_out.append(jnp.exp2(m_prev - m_new))
            m_out.append(m_new)
        pv = jnp.dot(vt, jnp.concatenate(p_out, axis=1), preferred_element_type=F32)
        m_ref[...] = jnp.concatenate(m_out, axis=1)
        acc_ref[...] = jnp.concatenate(a_out, axis=1) * acc_all + pv

    for_chunk_pairs(att_step)
    _transposed_heads_store(o_ref, acc_ref[0:DSA_DIM, :] / acc_ref[DSA_DIM:DSA_DIM + 1, :], DSA_HEADS, DSA_DIM, tq)


def _dsa_call(dq, iq, iwt, dk, dvt, ik):
    B, H, T, _ = dq.shape
    tq, tk = TQ_DSA, TK_DSA
    cols = H * tq
    return pl.pallas_call(
        _dsa_kernel,
        grid=(B, T // tq),
        in_specs=[
            pl.BlockSpec((1, H, tq, LANE), lambda b, i: (b, 0, i, 0)),
            pl.BlockSpec((1, H, tq, IDX_DIM), lambda b, i: (b, 0, i, 0)),
            pl.BlockSpec((1, IDX_HEADS, tq), lambda b, i: (b, 0, i)),
            pl.BlockSpec((1, T, LANE), lambda b, i: (b, 0, 0)),
            pl.BlockSpec((1, T // tk, V_AUG, tk), lambda b, i: (b, 0, 0, 0)),
            pl.BlockSpec((1, T, IDX_DIM), lambda b, i: (b, 0, 0)),
        ],
        out_specs=pl.BlockSpec((1, H * DSA_DIM, tq), lambda b, i: (b, 0, i)),
        out_shape=jax.ShapeDtypeStruct((B, H * DSA_DIM, T), BF16),
        scratch_shapes=[
            pltpu.VMEM((T // tk, tk, tq), F32),
            pltpu.VMEM((1, cols), F32),
            pltpu.VMEM((V_AUG, cols), F32),
        ],
        compiler_params=pltpu.CompilerParams(dimension_semantics=("parallel", "arbitrary")),
        name="dsa",
    )(dq, iq, iwt, dk, dvt, ik)


def _route(aff, biased):
    b = [biased[e:e + 1] for e in range(N_EXPERTS)]
    a = [aff[e:e + 1] for e in range(N_EXPERTS)]
    gs = []
    for g in range(N_GROUPS):
        v0, v1, v2, v3 = b[4 * g:4 * g + 4]
        hi1, lo1 = jnp.maximum(v0, v1), jnp.minimum(v0, v1)
        hi2, lo2 = jnp.maximum(v2, v3), jnp.minimum(v2, v3)
        gs.append(jnp.maximum(hi1, hi2) + jnp.maximum(jnp.minimum(hi1, hi2), jnp.maximum(lo1, lo2)))
    best, sel = gs[0], jnp.zeros_like(gs[0], dtype=I32)
    for g in range(1, N_GROUPS):
        upd = gs[g] > best
        sel = jnp.where(upd, g, sel)
        best = jnp.where(upd, gs[g], best)
    v, av = [], []
    for k in range(EXPERTS_PER_GROUP):
        vk, ak = b[k], a[k]
        for g in range(1, N_GROUPS):
            vk = jnp.where(sel == g, b[4 * g + k], vk)
            ak = jnp.where(sel == g, a[4 * g + k], ak)
        v.append(vk)
        av.append(ak)
    m1, i1 = v[0], jnp.zeros_like(sel)
    for k in range(1, EXPERTS_PER_GROUP):
        upd = v[k] > m1
        i1 = jnp.where(upd, k, i1)
        m1 = jnp.where(upd, v[k], m1)
    m2, i2 = jnp.full_like(m1, -jnp.inf), jnp.zeros_like(sel)
    for k in range(EXPERTS_PER_GROUP):
        cand = jnp.where(i1 == k, -jnp.inf, v[k])
        upd = cand > m2
        i2 = jnp.where(upd, k, i2)
        m2 = jnp.where(upd, cand, m2)
    a1, a2 = jnp.zeros_like(m1), jnp.zeros_like(m1)
    for k in range(EXPERTS_PER_GROUP):
        a1 = jnp.where(i1 == k, av[k], a1)
        a2 = jnp.where(i2 == k, av[k], a2)
    tot = a1 + a2
    e1 = sel * EXPERTS_PER_GROUP + i1
    e2 = sel * EXPERTS_PER_GROUP + i2
    return e1, e2, a1 / tot, a2 / tot


def _out_kernel(a_ref, b_ref, x_ref, wo_ref, g_ref, beta_ref, rwh_ref, rwl_ref, rb_ref, x1_ref, route_ref):
    half = MLA_HEADS * MLA_V
    rwh, rwl = rwh_ref[...], rwl_ref[...]
    for s in range(TM_OUT // TM_OUT_SUB):
        rows = slice(TM_OUT_SUB * s, TM_OUT_SUB * (s + 1))
        tn = (((0,), (0,)), ((), ()))
        mix = lax.dot_general(a_ref[0, :, rows], wo_ref[0:half, :], tn, preferred_element_type=F32)
        mix = mix + lax.dot_general(b_ref[0, :, rows], wo_ref[half:2 * half, :], tn, preferred_element_type=F32)
        x1 = _layer_norm(ALPHA * x_ref[0, rows, :] + mix, g_ref[...], beta_ref[...])
        x1_ref[0, rows, :] = x1
        xh = x1.astype(BF16)
        xl = (x1 - xh.astype(F32)).astype(BF16)
        logits = lax.dot_general(rwh, xh, NT_DIMS, preferred_element_type=F32)
        logits = logits + lax.dot_general(rwh, xl, NT_DIMS, preferred_element_type=F32)
        logits = logits + lax.dot_general(rwl, xh, NT_DIMS, preferred_element_type=F32)
        aff = 1.0 / (1.0 + jnp.exp(-logits))
        e1, e2, w1, w2 = _route(aff, aff + rb_ref[...])
        info = [e1.astype(F32), e2.astype(F32), w1, w2, jnp.zeros((LANE - 4, TM_OUT_SUB), F32)]
        route_ref[0, rows, :] = jnp.concatenate(info, axis=0).T


def _out_call(a, b, x, wo, g, beta, rwh, rwl, rb):
    B, T, _ = x.shape
    tm = TM_OUT
    full = lambda shape: pl.BlockSpec(shape, lambda bb, t: (0,) * len(shape))
    tok = lambda d: pl.BlockSpec((1, tm, d), lambda bb, t: (bb, t, 0))
    return pl.pallas_call(
        _out_kernel,
        grid=(B, T // tm),
        in_specs=[pl.BlockSpec((1, a.shape[1], tm), lambda bb, t: (bb, 0, t)),
                  pl.BlockSpec((1, b.shape[1], tm), lambda bb, t: (bb, 0, t)),
                  tok(D_MODEL), full(wo.shape), full(g.shape), full(beta.shape),
                  full(rwh.shape), full(rwl.shape), full(rb.shape)],
        out_specs=[tok(D_MODEL), tok(LANE)],
        out_shape=[jax.ShapeDtypeStruct((B, T, D_MODEL), F32), jax.ShapeDtypeStruct((B, T, LANE), F32)],
        compiler_params=pltpu.CompilerParams(dimension_semantics=("parallel", "parallel")),
        name="outproj",
    )(a, b, x, wo, g, beta, rwh, rwl, rb)


def _route_positions(route, n_tiles):
    eid = route[:, 0:TOP_K].astype(I32).reshape(-1)
    onehot = (eid[:, None] == jnp.arange(N_EXPERTS, dtype=I32)[None, :]).astype(I32)
    csum = jnp.cumsum(onehot, axis=0)
    rank = jnp.sum(onehot * csum, axis=1) - 1
    counts = csum[-1]
    padded = ((counts + TR_MOE - 1) // TR_MOE) * TR_MOE
    ends = jnp.cumsum(padded)
    starts = ends - padded
    pos = jnp.sum(onehot * starts[None, :], axis=1) + rank
    n_active = ends[-1] // TR_MOE
    tile_row = jnp.minimum(jnp.arange(n_tiles, dtype=I32), n_active - 1) * TR_MOE
    tile_expert = jnp.sum((tile_row[:, None] >= ends[None, :]).astype(I32), axis=1)
    last_tile = jnp.where(padded > counts, ends // TR_MOE - 1, -1)
    spare = n_active + jnp.arange(N_EXPERTS, dtype=I32)
    fill = jnp.concatenate([last_tile, jnp.where(spare < n_tiles, spare, -1)])
    return pos.astype(I32), tile_expert.astype(I32), n_active.reshape(1).astype(I32), fill.astype(I32)


def _row_copy_loop(tm, copy):
    def body(q, carry):
        base = pl.multiple_of(q * SUBLANES, SUBLANES)
        for j in range(SUBLANES):
            for k in range(TOP_K):
                copy(base + j, (base + j) * TOP_K + k, k).start(priority=k)
        return carry

    lax.fori_loop(0, tm // SUBLANES, body, 0)


def _dispatch_kernel(pos_ref, fill_ref, x_ref, xs_ref, xr_ref, xr2_ref, sem, sem2, zsem):
    tm = x_ref.shape[0]

    @pl.when(pl.program_id(0) == 0)
    def _():
        xr_ref[...] = jnp.zeros(xr_ref.shape, F32)

        def zero_copy(j):
            start = pl.multiple_of(jnp.maximum(fill_ref[j], 0) * tm, tm)
            return pltpu.make_async_copy(xr_ref, xs_ref.at[pl.ds(start, tm)], zsem)

        for j in range(fill_ref.shape[0]):
            @pl.when(fill_ref[j] >= 0)
            def _():
                zero_copy(j).start()

        for j in range(fill_ref.shape[0]):
            @pl.when(fill_ref[j] >= 0)
            def _():
                zero_copy(j).wait()

    i = pl.program_id(0)
    n_steps = pl.num_programs(0)

    def wait_tile(buf, s):
        for k in range(TOP_K):
            pltpu.make_async_copy(buf, xs_ref.at[pl.ds(0, tm)], s).wait()

    def step(cur, sem_cur, prev, sem_prev):
        cur[...] = x_ref[...].reshape(tm, SUBLANES, LANE)
        _row_copy_loop(tm, lambda r, p, k: pltpu.make_async_copy(cur.at[r], xs_ref.at[pos_ref[p]], sem_cur))

        @pl.when(i > 0)
        def _():
            wait_tile(prev, sem_prev)

        @pl.when(i == n_steps - 1)
        def _():
            wait_tile(cur, sem_cur)

    @pl.when((i & 1) == 0)
    def _():
        step(xr_ref, sem, xr2_ref, sem2)

    @pl.when((i & 1) == 1)
    def _():
        step(xr2_ref, sem2, xr_ref, sem)


def _dispatch_call(pos, fill, x1, n_tiles):
    n = x1.shape[0]
    tm = TM_DISP
    assert tm == TR_MOE
    return pl.pallas_call(
        _dispatch_kernel,
        grid=(n // tm,),
        in_specs=[
            pl.BlockSpec((TOP_K * tm,), lambda i: (i,), memory_space=pltpu.SMEM),
            pl.BlockSpec(fill.shape, lambda i: (0,), memory_space=pltpu.SMEM),
            pl.BlockSpec((tm, D_MODEL), lambda i: (i, 0)),
        ],
        out_specs=pl.BlockSpec(memory_space=pl.ANY),
        out_shape=jax.ShapeDtypeStruct((n_tiles * TR_MOE, SUBLANES, LANE), F32),
        scratch_shapes=[pltpu.VMEM((tm, SUBLANES, LANE), F32), pltpu.VMEM((tm, SUBLANES, LANE), F32),
                        pltpu.SemaphoreType.DMA(()), pltpu.SemaphoreType.DMA(()), pltpu.SemaphoreType.DMA(())],
        compiler_params=pltpu.CompilerParams(dimension_semantics=("arbitrary",)),
        name="dispatch",
    )(pos, fill, x1)


def _expert_kernel(te_ref, na_ref, xs_ref, wg_ref, wu_ref, wd_ref, ys_ref):
    del te_ref
    active = pl.program_id(0) < na_ref[0]

    @pl.when(active)
    def _():
        tr = xs_ref.shape[0]
        xb = xs_ref[...].reshape(tr, D_MODEL).astype(BF16)
        hg = jnp.dot(xb, wg_ref[0, 0].astype(BF16), preferred_element_type=F32)
        hu = jnp.dot(xb, wu_ref[0, 0].astype(BF16), preferred_element_type=F32)
        act = (hg / (1.0 + jnp.exp(-hg))) * hu
        y = jnp.dot(act.astype(BF16), wd_ref[0, 0].astype(BF16), preferred_element_type=F32)
        ys_ref[...] = y.reshape(tr, SUBLANES, LANE)

    @pl.when(jnp.logical_not(active))
    def _():
        ys_ref[...] = jnp.zeros_like(ys_ref)


def _expert_call(tile_expert, n_active, xs, wg, wu, wd, layer):
    rows = xs.shape[0]
    tr = TR_MOE
    row_map = lambda i, te, na: (jnp.minimum(i, na[0] - 1), 0, 0)
    out_map = lambda i, te, na: (i, 0, 0)
    w_map = lambda i, te, na: (layer, te[i], 0, 0)
    return pl.pallas_call(
        _expert_kernel,
        grid_spec=pltpu.PrefetchScalarGridSpec(
            num_scalar_prefetch=2,
            grid=(rows // tr,),
            in_specs=[
                pl.BlockSpec((tr, SUBLANES, LANE), row_map),
                pl.BlockSpec((1, 1, D_MODEL, D_FF), w_map),
                pl.BlockSpec((1, 1, D_MODEL, D_FF), w_map),
                pl.BlockSpec((1, 1, D_FF, D_MODEL), w_map),
            ],
            out_specs=pl.BlockSpec((tr, SUBLANES, LANE), out_map),
        ),
        out_shape=jax.ShapeDtypeStruct(xs.shape, F32),
        compiler_params=pltpu.CompilerParams(dimension_semantics=("arbitrary",)),
        name="experts",
    )(tile_expert, n_active, xs, wg, wu, wd)


def _combine_kernel(pos_ref, posn_ref, x1_ref, route_ref, ys_ref, g_ref, beta_ref, o_ref, buf_a, buf_b, sem_a, sem_b):
    i = pl.program_id(0)
    n_steps = pl.num_programs(0)
    tm = x1_ref.shape[0]

    def row_copy(idx_ref, p, buf, k, r, sem):
        return pltpu.make_async_copy(ys_ref.at[idx_ref[p]], buf.at[k, r], sem)

    def wait_tile(buf, sem):
        for k in range(TOP_K):
            pltpu.make_async_copy(ys_ref.at[pl.ds(0, tm)], buf.at[k], sem).wait()

    @pl.when(i == 0)
    def _():
        _row_copy_loop(tm, lambda r, p, k: row_copy(pos_ref, p, buf_a, k, r, sem_a))

    def step(cur, sem_cur, nxt, sem_nxt):
        wait_tile(cur, sem_cur)
        for r in range(tm):
            for k in range(TOP_K):
                row_copy(posn_ref, TOP_K * r + k, nxt, k, r, sem_nxt).start(priority=k)
        rt = route_ref[...]
        ffn = rt[:, TOP_K:TOP_K + 1] * cur[0].reshape(tm, D_MODEL)
        for k in range(1, TOP_K):
            ffn = ffn + rt[:, TOP_K + k:TOP_K + k + 1] * cur[k].reshape(tm, D_MODEL)
        o_ref[...] = _layer_norm(ALPHA * x1_ref[...] + ffn, g_ref[...], beta_ref[...])

        @pl.when(i == n_steps - 1)
        def _():
            wait_tile(nxt, sem_nxt)

    @pl.when((i & 1) == 0)
    def _():
        step(buf_a, sem_a, buf_b, sem_b)

    @pl.when((i & 1) == 1)
    def _():
        step(buf_b, sem_b, buf_a, sem_a)


def _combine_call(pos, x1, route, ys, g, beta):
    n = x1.shape[0]
    tm = TM_DISP
    n_steps = n // tm
    vec = pl.BlockSpec((1, D_MODEL), lambda i: (0, 0))
    buf = pltpu.VMEM((TOP_K, tm, SUBLANES, LANE), F32)
    return pl.pallas_call(
        _combine_kernel,
        grid=(n_steps,),
        in_specs=[
            pl.BlockSpec((TOP_K * tm,), lambda i: (i,), memory_space=pltpu.SMEM),
            pl.BlockSpec((TOP_K * tm,), lambda i: (jnp.minimum(i + 1, n_steps - 1),), memory_space=pltpu.SMEM),
            pl.BlockSpec((tm, D_MODEL), lambda i: (i, 0)),
            pl.BlockSpec((tm, LANE), lambda i: (i, 0)),
            pl.BlockSpec(memory_space=pl.ANY),
            vec, vec,
        ],
        out_specs=pl.BlockSpec((tm, D_MODEL), lambda i: (i, 0)),
        out_shape=jax.ShapeDtypeStruct((n, D_MODEL), F32),
        scratch_shapes=[buf, buf, pltpu.SemaphoreType.DMA(()), pltpu.SemaphoreType.DMA(())],
        compiler_params=pltpu.CompilerParams(dimension_semantics=("arbitrary",)),
        name="combine",
    )(pos, pos, x1, route, ys, g, beta)


def _moe_call(x1, route, wg, wu, wd, layer, g, beta):
    n = x1.shape[0]
    n_tiles = TOP_K * n // TR_MOE + N_EXPERTS
    pos, tile_expert, n_active, fill = _route_positions(route, n_tiles)
    xs = _dispatch_call(pos, fill, x1, n_tiles)
    ys = _expert_call(tile_expert, n_active, xs, wg, wu, wd, layer)
    return _combine_call(pos, x1, route, ys, g, beta)


def _rope_tables(T):
    half = MLA_ROPE // 2
    pos = jnp.arange(T, dtype=F32)
    inv = ROPE_BASE ** (-jnp.arange(half, dtype=F32) / half)
    ang = pos[:, None] * inv[None, :]
    cos, sin = jnp.cos(ang), jnp.sin(ang)
    one = jnp.ones((T, MLA_NOPE), F32)
    z16 = jnp.zeros((T, half), F32)
    z32 = jnp.zeros((T, LANE - MLA_NOPE - MLA_ROPE), F32)
    z64 = jnp.zeros((T, MLA_NOPE), F32)
    rc = jnp.concatenate([one, cos, cos, z32], axis=1)
    rs1 = jnp.concatenate([z64, z16, sin, z32], axis=1)
    rs2 = jnp.concatenate([z64, -sin, z16, z32], axis=1)
    return rc, rs1, rs2


def _alibi_features(T):
    c = jnp.asarray(SLOPES, F32) * LOG2E
    c1 = c.astype(BF16).astype(F32)
    c2 = (c - c1).astype(BF16).astype(F32)
    c3 = (c - c1 - c2).astype(BF16).astype(F32)
    width = LANE - DSA_DIM
    qfeat = jnp.stack([c1, c1, c2, c2, c3, c3], axis=1)
    qfeat = jnp.pad(qfeat, ((0, 0), (0, width - qfeat.shape[1])))
    pos = jnp.arange(T, dtype=I32)
    hi = ((pos // 256) * 256).astype(F32)
    lo = (pos % 256).astype(F32)
    kfeat = jnp.stack([hi, lo, hi, lo, hi, lo], axis=1)
    kfeat = jnp.pad(kfeat, ((0, 0), (0, width - kfeat.shape[1])))
    return qfeat, kfeat


def _pack_w_in(w):
    sizes = (Q_LORA, KV_LORA, MLA_ROPE, DSA_HEADS * DSA_DIM, DSA_DIM, DSA_DIM, IDX_HEADS * IDX_DIM, IDX_DIM, IDX_HEADS)
    offs = np.concatenate([[0], np.cumsum(sizes)])
    w = w.astype(BF16)
    qa, kva, kr, dq, dk, dv, iq, ik, iw = [w[:, int(offs[j]):int(offs[j + 1])] for j in range(len(sizes))]
    z = lambda n: jnp.zeros((w.shape[0], n), w.dtype)
    cat = jnp.concatenate([
        qa, kva, dq, iq,
        z(MLA_NOPE), kr, z(LANE - MLA_NOPE - MLA_ROPE),
        dk, ik, z(LANE - DSA_DIM - IDX_DIM),
    ], axis=1)
    assert cat.shape[1] == C_END
    side = jnp.concatenate([dv, iw, z(R_END - DSA_DIM - IDX_HEADS)], axis=1).T
    return cat, side


def kernel(x, w_in, q_norm_g, w_q_up, kv_norm_g, w_uk, w_uv, w_o, ln1_g, ln1_b, router_w, router_bias,
           w_gate, w_up, w_down, ln2_g, ln2_b):
    B, T, D = x.shape
    rc, rs1, rs2 = _rope_tables(T)
    rwt = router_w.T
    rwh = rwt.astype(BF16)
    rwl = (rwt - rwh.astype(F32)).astype(BF16)
    rb = router_bias.reshape(N_EXPERTS, 1).astype(F32)
    qfeat, kfeat = _alibi_features(T)
    for l in range(DEPTH):
        wcat, wside = _pack_w_in(w_in[l])
        wq = w_q_up[l].astype(BF16).reshape(Q_LORA, MLA_HEADS, MLA_NOPE + MLA_ROPE)
        wq = jnp.pad(wq, ((0, 0), (0, 0), (0, HEAD_PAD - MLA_NOPE - MLA_ROPE)))
        wq = wq.reshape(Q_LORA, MLA_HEADS // 2, 2 * HEAD_PAD).transpose(1, 0, 2)
        wk = w_uk[l].astype(BF16).reshape(KV_LORA, MLA_HEADS, MLA_NOPE)
        wk = jnp.pad(wk, ((0, 0), (0, 0), (0, HEAD_PAD - MLA_NOPE)))
        wk = wk.reshape(KV_LORA, MLA_HEADS // 2, 2 * HEAD_PAD).transpose(1, 0, 2)
        wvt = w_uv[l].astype(BF16).T
        q, k, vt, dq, dk, dvt, iq, ik, iwt = _proj_call(
            x, wcat, wside, q_norm_g[l].reshape(1, Q_LORA), kv_norm_g[l].reshape(1, KV_LORA), wq, wk, wvt,
            rc, rs1, rs2, qfeat, kfeat)
        out_a = _mla_call(q, k, vt)
        out_b = _dsa_call(dq, iq, iwt, dk, dvt, ik)
        x1, route = _out_call(out_a, out_b, x, w_o[l].astype(BF16), ln1_g[l].reshape(1, D), ln1_b[l].reshape(1, D),
                              rwh, rwl, rb)
        y = _moe_call(x1.reshape(B * T, D), route.reshape(B * T, LANE), w_gate, w_up,
                      w_down, l, ln2_g[l].reshape(1, D), ln2_b[l].reshape(1, D))
        x = y.reshape(B, T, D)
    return x
```
